```python
import math
import jax, jax.numpy as jnp
from jax import lax
import numpy as np

D_MODEL = 2048
BATCH = 4
SEQ = 4096
DEPTH = 4

N_MIXERS = 3
EPS = 1e-6
POOL_WINDOWS = (2, 4, 8, 16)
N_POOL_GROUPS = len(POOL_WINDOWS)
POOL_GROUP = D_MODEL // N_POOL_GROUPS
SWA_HEADS = 32
SWA_KV_HEADS = 4
SWA_GROUP = SWA_HEADS // SWA_KV_HEADS
SWA_HEAD_DIM = D_MODEL // SWA_HEADS
SWA_WINDOW = 128
BLOCK = 128
MLA_HEADS = 16
MLA_NOPE = 128
MLA_ROPE = 64
MLA_V = 128
MLA_Q_RANK = 512
MLA_KV_RANK = 512
ROPE_THETA = 10000.0
D_FF = 5632
CONV_W = 3

kernel_name = "hybrid_pool_swa_mla_convglu_encoder"


def rmsnorm(x, g):
    xf = x.astype(jnp.float32)
    y = xf * lax.rsqrt(jnp.mean(xf * xf, axis=-1, keepdims=True) + EPS)
    return (y * g.astype(jnp.float32)).astype(x.dtype)


def alibi_slopes(n):
    return jnp.asarray(2.0 ** (-8.0 * np.arange(1, n + 1) / n), dtype=jnp.float32)


def rope_tables(positions, dim):
    inv = ROPE_THETA ** (-jnp.arange(0, dim, 2, dtype=jnp.float32) / dim)
    ang = positions.astype(jnp.float32)[:, None] * inv[None, :]
    return jnp.cos(ang), jnp.sin(ang)


def apply_rope(x, cos, sin):
    x1, x2 = jnp.split(x.astype(jnp.float32), 2, axis=-1)
    return jnp.concatenate([x1 * cos - x2 * sin, x2 * cos + x1 * sin], axis=-1).astype(x.dtype)


def pool_mixer(h, w_groups, scale):
    B, S, D = h.shape
    hf = h.astype(jnp.float32).reshape(B, S, N_POOL_GROUPS, POOL_GROUP)
    csum = jnp.concatenate(
        [jnp.zeros((B, 1, N_POOL_GROUPS, POOL_GROUP), jnp.float32), jnp.cumsum(hf, axis=1)],
        axis=1)
    left = np.array([w // 2 for w in POOL_WINDOWS], dtype=np.int32)
    right = np.array([w - 1 - w // 2 for w in POOL_WINDOWS], dtype=np.int32)
    t = jnp.arange(S, dtype=jnp.int32)[:, None]
    hi = jnp.clip(t + right[None, :] + 1, 0, S)
    lo = jnp.clip(t - left[None, :], 0, S)
    g_idx = jnp.arange(N_POOL_GROUPS)[None, :]
    win_sum = csum[:, hi, g_idx] - csum[:, lo, g_idx]
    count = (hi - lo).astype(jnp.float32)[None, :, :, None]
    pooled = (win_sum / count - hf).astype(h.dtype)
    y = jnp.einsum('bsgc,gcd->bsgd', pooled, w_groups).reshape(B, S, D)
    return y * scale


def swa_mixer(h, positions, w_qkv, q_gain, k_gain, sinks, w_o):
    B, S, D = h.shape
    nq = SWA_HEADS * SWA_HEAD_DIM
    nkv = SWA_KV_HEADS * SWA_HEAD_DIM
    qkv = h @ w_qkv
    q = qkv[..., :nq].reshape(B, S, SWA_KV_HEADS, SWA_GROUP, SWA_HEAD_DIM)
    k = qkv[..., nq:nq + nkv].reshape(B, S, SWA_KV_HEADS, SWA_HEAD_DIM)
    v = qkv[..., nq + nkv:].reshape(B, S, SWA_KV_HEADS, SWA_HEAD_DIM)
    q = rmsnorm(q, q_gain) * (SWA_HEAD_DIM ** -0.5)
    k = rmsnorm(k, k_gain)
    pad = SWA_WINDOW
    span = BLOCK + 2 * SWA_WINDOW
    k_pad = jnp.pad(k, ((0, 0), (pad, pad), (0, 0), (0, 0)))
    v_pad = jnp.pad(v, ((0, 0), (pad, pad), (0, 0), (0, 0)))
    pos_pad = jnp.pad(positions, (pad, pad))
    valid_pad = jnp.pad(jnp.ones((S,), dtype=bool), (pad, pad))
    slopes = alibi_slopes(SWA_HEADS).reshape(SWA_KV_HEADS, SWA_GROUP)
    sink = sinks.astype(jnp.float32).reshape(SWA_KV_HEADS, SWA_GROUP)[None, :, :, None]

    def block(j):
        start = j * BLOCK
        qb = lax.dynamic_slice_in_dim(q, start, BLOCK, axis=1)
        kb = lax.dynamic_slice_in_dim(k_pad, start, span, axis=1)
        vb = lax.dynamic_slice_in_dim(v_pad, start, span, axis=1)
        pq = lax.dynamic_slice_in_dim(positions, start, BLOCK)
        pk = lax.dynamic_slice_in_dim(pos_pad, start, span)
        ok = lax.dynamic_slice_in_dim(valid_pad, start, span)
        qi = start + jnp.arange(BLOCK)
        ki = start - SWA_WINDOW + jnp.arange(span)
        in_win = (jnp.abs(qi[:, None] - ki[None, :]) <= SWA_WINDOW) & ok[None, :]
        s = jnp.einsum('bqkgd,bskd->bkgqs', qb, kb).astype(jnp.float32)
        dist = jnp.abs(pq[:, None] - pk[None, :]).astype(jnp.float32)
        s = s - slopes[:, :, None, None] * dist
        s = jnp.where(in_win, s, -jnp.inf)
        m = jnp.maximum(jnp.max(s, axis=-1), sink)
        p = jnp.exp(s - m[..., None])
        denom = jnp.sum(p, axis=-1) + jnp.exp(sink - m)
        p = (p / denom[..., None]).astype(vb.dtype)
        return jnp.einsum('bkgqs,bskd->bqkgd', p, vb)

    o = lax.map(block, jnp.arange(S // BLOCK))
    o = jnp.transpose(o, (1, 0, 2, 3, 4, 5)).reshape(B, S, nq)
    return o @ w_o


def mla_mixer(h, positions, w_down, q_a_gain, kv_a_gain, w_uq, w_ukv,
              qn_gain, qr_gain, kn_gain, kr_gain, w_o):
    B, S, D = h.shape
    d = h @ w_down
    cq = rmsnorm(d[..., :MLA_Q_RANK], q_a_gain)
    ckv = rmsnorm(d[..., MLA_Q_RANK:MLA_Q_RANK + MLA_KV_RANK], kv_a_gain)
    k_pe = d[..., MLA_Q_RANK + MLA_KV_RANK:]
    q = (cq @ w_uq).reshape(B, S, MLA_HEADS, MLA_NOPE + MLA_ROPE)
    kv = (ckv @ w_ukv).reshape(B, S, MLA_HEADS, MLA_NOPE + MLA_V)
    q_nope = rmsnorm(q[..., :MLA_NOPE], qn_gain)
    q_pe = rmsnorm(q[..., MLA_NOPE:], qr_gain)
    k_nope = rmsnorm(kv[..., :MLA_NOPE], kn_gain)
    v = kv[..., MLA_NOPE:]
    k_pe = rmsnorm(k_pe, kr_gain)
    cos, sin = rope_tables(positions, MLA_ROPE)
    q_pe = apply_rope(q_pe, cos[:, None, :], sin[:, None, :])
    k_pe = apply_rope(k_pe, cos, sin)
    scale = (MLA_NOPE + MLA_ROPE) ** -0.5
    q_nope = q_nope * scale
    q_pe = q_pe * scale

    def block(j):
        start = j * BLOCK
        qn = lax.dynamic_slice_in_dim(q_nope, start, BLOCK, axis=1)
        qp = lax.dynamic_slice_in_dim(q_pe, start, BLOCK, axis=1)
        s = (jnp.einsum('bqhd,bshd->bhqs', qn, k_nope).astype(jnp.float32)
             + jnp.einsum('bqhr,bsr->bhqs', qp, k_pe).astype(jnp.float32))
        p = jax.nn.softmax(s, axis=-1).astype(v.dtype)
        return jnp.einsum('bhqs,bshd->bqhd', p, v)

    o = lax.map(block, jnp.arange(S // BLOCK))
    o = jnp.transpose(o, (1, 0, 2, 3, 4)).reshape(B, S, MLA_HEADS * MLA_V)
    return o @ w_o


def conv_glu(h, w_in, conv_w, conv_b, w_out):
    u = h @ w_in
    g, val = u[..., :D_FF], u[..., D_FF:]
    gp = jnp.pad(g, ((0, 0), (1, 1), (0, 0)))
    g = gp[:, :-2] * conv_w[0] + gp[:, 1:-1] * conv_w[1] + gp[:, 2:] * conv_w[2] + conv_b
    return (jax.nn.silu(g) * val) @ w_out


def setup_inputs(seed: int = 0) -> dict:
    key = jax.random.key(seed)
    ks = iter(jax.random.split(key, 40))
    f32 = jnp.float32
    n_pool = (DEPTH + 2) // 3
    n_swa = (DEPTH + 1) // 3
    n_mla = DEPTH // 3
    res = (2.0 * DEPTH) ** -0.5

    def w(shape, fan_in, gain=1.0):
        return jax.random.normal(next(ks), shape, f32) * (gain * fan_in ** -0.5)

    def gain(shape):
        return 1.0 + 0.02 * jax.random.normal(next(ks), shape, f32)

    x = jax.random.normal(next(ks), (BATCH, SEQ, D_MODEL), f32)
    positions = jnp.arange(SEQ, dtype=jnp.int32)
    norm_mix_g = gain((DEPTH, D_MODEL))
    norm_ffn_g = gain((DEPTH, D_MODEL))
    pool_w = w((n_pool, N_POOL_GROUPS, POOL_GROUP, POOL_GROUP), POOL_GROUP, res)
    pool_scale = 1.0 + 0.1 * jax.random.normal(next(ks), (n_pool, D_MODEL), f32)
    n_qkv = SWA_HEADS * SWA_HEAD_DIM + 2 * SWA_KV_HEADS * SWA_HEAD_DIM
    swa_w_qkv = w((n_swa, D_MODEL, n_qkv), D_MODEL)
    swa_q_gain = gain((n_swa, SWA_HEAD_DIM))
    swa_k_gain = gain((n_swa, SWA_HEAD_DIM))
    swa_sinks = 0.5 * jax.random.normal(next(ks), (n_swa, SWA_HEADS), f32)
    swa_w_o = w((n_swa, SWA_HEADS * SWA_HEAD_DIM, D_MODEL), SWA_HEADS * SWA_HEAD_DIM, res)
    mla_w_down = w((n_mla, D_MODEL, MLA_Q_RANK + MLA_KV_RANK + MLA_ROPE), D_MODEL)
    mla_q_a_gain = gain((n_mla, MLA_Q_RANK))
    mla_kv_a_gain = gain((n_mla, MLA_KV_RANK))
    mla_w_uq = w((n_mla, MLA_Q_RANK, MLA_HEADS * (MLA_NOPE + MLA_ROPE)), MLA_Q_RANK)
    mla_w_ukv = w((n_mla, MLA_KV_RANK, MLA_HEADS * (MLA_NOPE + MLA_V)), MLA_KV_RANK)
    mla_qn_gain = gain((n_mla, MLA_NOPE))
    mla_qr_gain = gain((n_mla, MLA_ROPE))
    mla_kn_gain = gain((n_mla, MLA_NOPE))
    mla_kr_gain = gain((n_mla, MLA_ROPE))
    mla_w_o = w((n_mla, MLA_HEADS * MLA_V, D_MODEL), MLA_HEADS * MLA_V, res)
    ffn_w_in = w((DEPTH, D_MODEL, 2 * D_FF), D_MODEL)
    ffn_conv_w = w((DEPTH, CONV_W, D_FF), CONV_W)
    ffn_conv_b = 0.02 * jax.random.normal(next(ks), (DEPTH, D_FF), f32)
    ffn_w_out = w((DEPTH, D_FF, D_MODEL), D_FF, res)
    return {
        "x": x, "positions": positions,
        "norm_mix_g": norm_mix_g, "norm_ffn_g": norm_ffn_g,
        "pool_w": pool_w, "pool_scale": pool_scale,
        "swa_w_qkv": swa_w_qkv, "swa_q_gain": swa_q_gain, "swa_k_gain": swa_k_gain,
        "swa_sinks": swa_sinks, "swa_w_o": swa_w_o,
        "mla_w_down": mla_w_down, "mla_q_a_gain": mla_q_a_gain, "mla_kv_a_gain": mla_kv_a_gain,
        "mla_w_uq": mla_w_uq, "mla_w_ukv": mla_w_ukv,
        "mla_qn_gain": mla_qn_gain, "mla_qr_gain": mla_qr_gain,
        "mla_kn_gain": mla_kn_gain, "mla_kr_gain": mla_kr_gain, "mla_w_o": mla_w_o,
        "ffn_w_in": ffn_w_in, "ffn_conv_w": ffn_conv_w, "ffn_conv_b": ffn_conv_b,
        "ffn_w_out": ffn_w_out,
    }


def reference(x, positions, norm_mix_g, norm_ffn_g, pool_w, pool_scale,
              swa_w_qkv, swa_q_gain, swa_k_gain, swa_sinks, swa_w_o,
              mla_w_down, mla_q_a_gain, mla_kv_a_gain, mla_w_uq, mla_w_ukv,
              mla_qn_gain, mla_qr_gain, mla_kn_gain, mla_kr_gain, mla_w_o,
              ffn_w_in, ffn_conv_w, ffn_conv_b, ffn_w_out):
    for i in range(DEPTH):
        kind = i % N_MIXERS
        j = i // N_MIXERS
        h = rmsnorm(x, norm_mix_g[i])
        if kind == 0:
            y = pool_mixer(h, pool_w[j], pool_scale[j])
        elif kind == 1:
            y = swa_mixer(h, positions, swa_w_qkv[j], swa_q_gain[j], swa_k_gain[j],
                          swa_sinks[j], swa_w_o[j])
        else:
            y = mla_mixer(h, positions, mla_w_down[j], mla_q_a_gain[j], mla_kv_a_gain[j],
                          mla_w_uq[j], mla_w_ukv[j], mla_qn_gain[j], mla_qr_gain[j],
                          mla_kn_gain[j], mla_kr_gain[j], mla_w_o[j])
        x = x + y
        h = rmsnorm(x, norm_ffn_g[i])
        x = x + conv_glu(h, ffn_w_in[i], ffn_conv_w[i], ffn_conv_b[i], ffn_w_out[i])
    return x
```

```python
import functools

import numpy as np
import jax
import jax.numpy as jnp
from jax import lax
from jax.experimental import pallas as pl
from jax.experimental.pallas import tpu as pltpu

F32 = jnp.float32
BF16 = jnp.bfloat16

EPS = 1e-6
N_MIXERS = 3
POOL_WINDOWS = (2, 4, 8, 16)
SWA_HEADS = 32
SWA_KV_HEADS = 4
SWA_GROUP = SWA_HEADS // SWA_KV_HEADS
SWA_HEAD_DIM = 64
SWA_WINDOW = 128
SWA_BLOCK = 128
SWA_SPAN = SWA_BLOCK + 2 * SWA_WINDOW
MLA_HEADS = 16
MLA_NOPE = 128
MLA_ROPE = 64
MLA_V = 128
MLA_Q_RANK = 512
MLA_KV_RANK = 512
MLA_SLOT = 256
ROPE_THETA = 10000.0

V7X_LANES = 128
V7X_SUBLANES = 8
V7X_VMEM_LIMIT_BYTES = 56 * 1024 * 1024


def _tiles(m, seq):
    def fit(t):
        while seq % t:
            t //= 2
        return t
    return dict(ffn_tm=fit(512), ffn_tf=512, pool_ts=fit(256), proj_tm=fit(512), up_tm=fit(256),
                attn_tq=fit(512), res_tm=fit(512))


def _params(*sem):
    return pltpu.CompilerParams(dimension_semantics=sem, vmem_limit_bytes=V7X_VMEM_LIMIT_BYTES)


def _rms(x, g):
    ms = jnp.mean(x * x, axis=-1, keepdims=True)
    return x * lax.rsqrt(ms + EPS) * g


def _halo_specs(tm, m, d, nargs):
    r = tm // V7X_SUBLANES
    last = m // V7X_SUBLANES - 1
    if nargs == 1:
        prev = pl.BlockSpec((V7X_SUBLANES, d), lambda i: (jnp.maximum(i * r - 1, 0), 0))
        nxt = pl.BlockSpec((V7X_SUBLANES, d), lambda i: (jnp.minimum((i + 1) * r, last), 0))
    else:
        prev = pl.BlockSpec((V7X_SUBLANES, d), lambda i, j: (jnp.maximum(i * r - 1, 0), 0))
        nxt = pl.BlockSpec((V7X_SUBLANES, d), lambda i, j: (jnp.minimum((i + 1) * r, last), 0))
    return prev, nxt


def _ffn_body(xp_ref, x_ref, xn_ref, g_ref, wg_ref, wv_ref, cwb_ref, wo_ref, o_ref, hn_ref, *, tm, seq):
    i = pl.program_id(0)
    j = pl.program_id(1)

    @pl.when(j == 0)
    def _():
        g = g_ref[...]
        x = x_ref[...]
        hn_ref[0:tm, :] = _rms(x, g).astype(BF16)
        s0 = (i * tm) % seq
        keep_prev = jnp.where(s0 != 0, 7, -1)
        keep_next = jnp.where(s0 + tm != seq, 8, -1)
        ext = jnp.concatenate([_rms(xp_ref[...], g), _rms(xn_ref[...], g)], axis=0)
        r = lax.broadcasted_iota(jnp.int32, (16, 1), 0)
        ext = jnp.where((r == keep_prev) | (r == keep_next), ext, 0.0)
        hn_ref[tm:tm + 16, :] = ext.astype(BF16)
        o_ref[...] = x

    ug = jnp.dot(hn_ref[...], wg_ref[...], preferred_element_type=F32)
    uv = jnp.dot(hn_ref[0:tm, :], wv_ref[...], preferred_element_type=F32)
    gm = ug[0:tm]
    g_prev = ug[tm + 7:tm + 8]
    g_next = ug[tm + 8:tm + 9]
    rows = lax.broadcasted_iota(jnp.int32, (tm, 1), 0)
    up = jnp.where(rows == 0, g_prev, pltpu.roll(gm, 1, 0))
    dn = jnp.where(rows == tm - 1, g_next, pltpu.roll(gm, tm - 1, 0))
    cwb = cwb_ref[...]
    gc = up * cwb[0:1] + gm * cwb[1:2] + dn * cwb[2:3] + cwb[3:4]
    act = (gc * jax.nn.sigmoid(gc) * uv).astype(BF16)
    o_ref[...] += jnp.dot(act, wo_ref[...], preferred_element_type=F32)


def _ffn_layer(x, gain, w_in, cwb, w_out, layer, seq, tm, tf):
    m, d = x.shape
    f = w_out.shape[1]
    nf = f // tf
    prev, nxt = _halo_specs(tm, m, d, 2)
    return pl.pallas_call(
        functools.partial(_ffn_body, tm=tm, seq=seq),
        grid=(m // tm, nf),
        in_specs=[
            prev,
            pl.BlockSpec((tm, d), lambda i, j: (i, 0)),
            nxt,
            pl.BlockSpec((1, d), lambda i, j: (0, 0)),
            pl.BlockSpec((None, d, tf), lambda i, j: (layer, 0, j)),
            pl.BlockSpec((None, d, tf), lambda i, j: (layer, 0, j + nf)),
            pl.BlockSpec((None, 8, tf), lambda i, j: (layer, 0, j)),
            pl.BlockSpec((None, tf, d), lambda i, j: (layer, j, 0)),
        ],
        out_specs=pl.BlockSpec((tm, d), lambda i, j: (i, 0)),
        out_shape=jax.ShapeDtypeStruct((m, d), F32),
        scratch_shapes=[pltpu.VMEM((tm + 16, d), BF16)],
        compiler_params=_params("arbitrary", "arbitrary"),
        name="ffn_convglu",
    )(x, x, x, gain, w_in, w_in, cwb, w_out)


def _pool_body(xp_ref, x_ref, xn_ref, g_ref, w_ref, sc_ref, o_ref, *, ts, seq):
    i = pl.program_id(0)
    g = g_ref[...]
    x = x_ref[...]
    s0 = (i * ts) % seq
    h = _rms(x, g)
    hp = _rms(xp_ref[...], g) * jnp.where(s0 != 0, 1.0, 0.0)
    hx = _rms(xn_ref[...], g) * jnp.where(s0 + ts != seq, 1.0, 0.0)
    n = ts + 16
    hext = jnp.concatenate([hp, h, hx], axis=0)
    pos = s0 + lax.broadcasted_iota(jnp.int32, (ts, 1), 0)
    dg = x.shape[1] // len(POOL_WINDOWS)
    for gi, w in enumerate(POOL_WINDOWS):
        left = w // 2
        right = w - 1 - left
        cols = slice(gi * dg, (gi + 1) * dg)
        p = hext[:, cols]
        k = 1
        while k < w:
            p = p + pltpu.roll(p, n - k, 0)
            k *= 2
        wsum = pltpu.roll(p, left, 0)[8:8 + ts]
        cnt = (jnp.minimum(pos + right + 1, seq) - jnp.maximum(pos - left, 0)).astype(F32)
        pooled = (wsum / cnt - h[:, cols]).astype(BF16)
        y = jnp.dot(pooled, w_ref[gi], preferred_element_type=F32) * sc_ref[:, cols]
        o_ref[:, cols] = x[:, cols] + y


def _pool_layer(x, gain, pool_w, scale, layer, seq, ts):
    m, d = x.shape
    ng, dg = pool_w.shape[1], pool_w.shape[2]
    prev, nxt = _halo_specs(ts, m, d, 1)
    return pl.pallas_call(
        functools.partial(_pool_body, ts=ts, seq=seq),
        grid=(m // ts,),
        in_specs=[
            prev,
            pl.BlockSpec((ts, d), lambda i: (i, 0)),
            nxt,
            pl.BlockSpec((1, d), lambda i: (0, 0)),
            pl.BlockSpec((None, ng, dg, dg), lambda i: (layer, 0, 0, 0)),
            pl.BlockSpec((1, d), lambda i: (0, 0)),
        ],
        out_specs=pl.BlockSpec((ts, d), lambda i: (i, 0)),
        out_shape=jax.ShapeDtypeStruct((m, d), F32),
        compiler_params=_params("arbitrary"),
        name="pool_mixer",
    )(x, x, x, gain, pool_w, scale)


def _proj_res_body(a_ref, w_ref, x_ref, o_ref):
    o_ref[...] = x_ref[...] + jnp.dot(a_ref[...], w_ref[...], preferred_element_type=F32)


def _proj_residual(a, w, x, tm):
    m, d = x.shape
    k = a.shape[1]
    return pl.pallas_call(
        _proj_res_body,
        grid=(m // tm,),
        in_specs=[
            pl.BlockSpec((tm, k), lambda i: (i, 0)),
            pl.BlockSpec((k, d), lambda i: (0, 0)),
            pl.BlockSpec((tm, d), lambda i: (i, 0)),
        ],
        out_specs=pl.BlockSpec((tm, d), lambda i: (i, 0)),
        out_shape=jax.ShapeDtypeStruct((m, d), F32),
        compiler_params=_params("arbitrary"),
        name="proj_residual",
    )(a, w, x)


def _group_sumsq(z, e):
    s = z * z
    s_hi = s.astype(BF16)
    s_lo = (s - s_hi.astype(F32)).astype(BF16)
    return (jnp.dot(s_hi, e, preferred_element_type=F32) + jnp.dot(s_lo, e, preferred_element_type=F32))


def _swa_qkv_body(x_ref, g_ref, w_ref, qg_ref, kg_ref, e_ref, q_ref, k_ref, v_ref):
    hn = _rms(x_ref[...], g_ref[...]).astype(BF16)
    qkv = jnp.dot(hn, w_ref[...], preferred_element_type=F32)
    e = e_ref[...]
    nq = q_ref.shape[1]
    nkv = k_ref.shape[1]
    cw = e.shape[0]
    inv_hd = 1.0 / SWA_HEAD_DIM

    def head_norm(z, gain):
        return z * lax.rsqrt(_group_sumsq(z, e) * inv_hd + EPS) * gain

    qg = qg_ref[...] * (SWA_HEAD_DIM ** -0.5)
    for c in range(nq // cw):
        cols = slice(c * cw, (c + 1) * cw)
        q_ref[:, cols] = head_norm(qkv[:, cols], qg).astype(BF16)
    k_ref[...] = head_norm(qkv[:, nq:nq + nkv], kg_ref[...]).astype(BF16)
    v_ref[...] = qkv[:, nq + nkv:].astype(BF16)


def _swa_qkv(x, gain, w_qkv, qg, kg, e, tm):
    m, d = x.shape
    nkv = SWA_KV_HEADS * SWA_HEAD_DIM
    nq = w_qkv.shape[1] - 2 * nkv
    return pl.pallas_call(
        _swa_qkv_body,
        grid=(m // tm,),
        in_specs=[
            pl.BlockSpec((tm, d), lambda i: (i, 0)),
            pl.BlockSpec((1, d), lambda i: (0, 0)),
            pl.BlockSpec(w_qkv.shape, lambda i: (0, 0)),
            pl.BlockSpec((1, nkv), lambda i: (0, 0)),
            pl.BlockSpec((1, nkv), lambda i: (0, 0)),
            pl.BlockSpec(e.shape, lambda i: (0, 0)),
        ],
        out_specs=[
            pl.BlockSpec((tm, nq), lambda i: (i, 0)),
            pl.BlockSpec((tm, nkv), lambda i: (i, 0)),
            pl.BlockSpec((tm, nkv), lambda i: (i, 0)),
        ],
        out_shape=[
            jax.ShapeDtypeStruct((m, nq), BF16),
            jax.ShapeDtypeStruct((m, nkv), BF16),
            jax.ShapeDtypeStruct((m, nkv), BF16),
        ],
        compiler_params=_params("arbitrary"),
        name="swa_qkv",
    )(x, gain, w_qkv, qg, kg, e)


def _swa_slopes():
    return [float(2.0 ** (-8.0 * (i + 1) / SWA_HEADS)) for i in range(SWA_HEADS)]


def _swa_attn_body(sink_ref, q_ref, k_ref, v_ref, pc_ref, pr_ref, o_ref, *, seq):
    j = pl.program_id(1)
    hd = SWA_HEAD_DIM
    start = pl.multiple_of(jnp.clip(j * SWA_BLOCK - SWA_WINDOW, 0, seq - SWA_SPAN), SWA_BLOCK)
    kblk = k_ref[0, pl.ds(start, SWA_SPAN), :]
    vblk = v_ref[0, pl.ds(start, SWA_SPAN), :]
    pk = pr_ref[:, pl.ds(start, SWA_SPAN)]
    pq = pc_ref[...]
    qi = j * SWA_BLOCK + lax.broadcasted_iota(jnp.int32, (SWA_BLOCK, 1), 0)
    ki = start + lax.broadcasted_iota(jnp.int32, (1, SWA_SPAN), 1)
    in_win = jnp.abs(qi - ki) <= SWA_WINDOW
    dist = jnp.where(in_win, jnp.abs(pq - pk).astype(F32), jnp.inf)
    dist2 = jnp.concatenate([dist, dist], axis=1)
    lane = lax.broadcasted_iota(jnp.int32, (1, 2 * hd), 1)
    lane2 = lax.broadcasted_iota(jnp.int32, (1, 2 * SWA_SPAN), 1)
    slopes = _swa_slopes()
    zero = jnp.zeros((), BF16)
    for mp in range(SWA_KV_HEADS // 2):
        kp = kblk[:, mp * 2 * hd:(mp + 1) * 2 * hd]
        vp = vblk[:, mp * 2 * hd:(mp + 1) * 2 * hd]
        k2 = jnp.concatenate([jnp.where(lane < hd, kp, zero), jnp.where(lane >= hd, kp, zero)], axis=0)
        v2 = jnp.concatenate([jnp.where(lane < hd, vp, zero), jnp.where(lane >= hd, vp, zero)], axis=0)
        for g in range(SWA_GROUP):
            ha = (2 * mp) * SWA_GROUP + g
            hb = (2 * mp + 1) * SWA_GROUP + g
            cols = slice((mp * SWA_GROUP + g) * 2 * hd, (mp * SWA_GROUP + g + 1) * 2 * hd)
            q2 = q_ref[0, :, cols]
            s = lax.dot_general(q2, k2, (((1,), (1,)), ((), ())), preferred_element_type=F32)
            s = s - jnp.where(lane2 < SWA_SPAN, slopes[ha], slopes[hb]) * dist2
            sa = s[:, :SWA_SPAN]
            sb = s[:, SWA_SPAN:]
            sink_a = sink_ref[ha]
            sink_b = sink_ref[hb]
            ma = jnp.maximum(jnp.max(sa, axis=-1, keepdims=True), sink_a)
            mb = jnp.maximum(jnp.max(sb, axis=-1, keepdims=True), sink_b)
            pa = jnp.exp(sa - ma)
            pb = jnp.exp(sb - mb)
            da = jnp.sum(pa, axis=-1, keepdims=True) + jnp.exp(sink_a - ma)
            db = jnp.sum(pb, axis=-1, keepdims=True) + jnp.exp(sink_b - mb)
            p2 = jnp.concatenate([pa, pb], axis=1).astype(BF16)
            o2 = jnp.dot(p2, v2, preferred_element_type=F32)
            o_ref[0, :, cols] = (o2 / jnp.where(lane < hd, da, db)).astype(BF16)


def _swa_attention(q, k, v, pos_col, pos_row, sinks, seq):
    b = q.shape[0]
    nq = q.shape[2]
    nkv = k.shape[2]
    return pl.pallas_call(
        functools.partial(_swa_attn_body, seq=seq),
        grid=(b, seq // SWA_BLOCK),
        in_specs=[
            pl.BlockSpec(memory_space=pltpu.SMEM),
            pl.BlockSpec((1, SWA_BLOCK, nq), lambda bi, j: (bi, j, 0)),
            pl.BlockSpec((1, seq, nkv), lambda bi, j: (bi, 0, 0)),
            pl.BlockSpec((1, seq, nkv), lambda bi, j: (bi, 0, 0)),
            pl.BlockSpec((SWA_BLOCK, 1), lambda bi, j: (j, 0)),
            pl.BlockSpec((1, seq), lambda bi, j: (0, 0)),
        ],
        out_specs=pl.BlockSpec((1, SWA_BLOCK, nq), lambda bi, j: (bi, j, 0)),
        out_shape=jax.ShapeDtypeStruct(q.shape, BF16),
        compiler_params=_params("arbitrary", "arbitrary"),
        name="swa_attention",
    )(sinks, q, k, v, pos_col, pos_row)


def _rope64(z, c, s1, s2):
    return z * c + pltpu.roll(z, V7X_LANES - MLA_ROPE // 2, 1) * s1 + pltpu.roll(z, MLA_ROPE // 2, 1) * s2


def _slot_rms(z, gain):
    ms = jnp.sum(z * z, axis=-1, keepdims=True) * (1.0 / MLA_ROPE)
    return z * lax.rsqrt(ms + EPS) * gain


def _mla_down_body(x_ref, g_ref, w_ref, qa_ref, kva_ref, kr_ref, c_ref, s1_ref, s2_ref,
                   cq_ref, ckv_ref, kpe_ref):
    hn = _rms(x_ref[...], g_ref[...]).astype(BF16)
    d = jnp.dot(hn, w_ref[...], preferred_element_type=F32)
    cq_ref[...] = _rms(d[:, :MLA_Q_RANK], qa_ref[...]).astype(BF16)
    ckv_ref[...] = _rms(d[:, MLA_Q_RANK:MLA_Q_RANK + MLA_KV_RANK], kva_ref[...]).astype(BF16)
    kp = _slot_rms(d[:, MLA_Q_RANK + MLA_KV_RANK:], kr_ref[...])
    kpe_ref[...] = _rope64(kp, c_ref[...], s1_ref[...], s2_ref[...]).astype(BF16)


def _mla_down(x, gain, w_down, qa, kva, kr, c, s1, s2, seq, tm):
    m, d = x.shape
    nt = seq // tm
    tab = pl.BlockSpec((tm, V7X_LANES), lambda i: (i % nt, 0))
    row = lambda n: pl.BlockSpec((1, n), lambda i: (0, 0))
    return pl.pallas_call(
        _mla_down_body,
        grid=(m // tm,),
        in_specs=[
            pl.BlockSpec((tm, d), lambda i: (i, 0)),
            row(d),
            pl.BlockSpec(w_down.shape, lambda i: (0, 0)),
            row(MLA_Q_RANK), row(MLA_KV_RANK), row(V7X_LANES),
            tab, tab, tab,
        ],
        out_specs=[
            pl.BlockSpec((tm, MLA_Q_RANK), lambda i: (i, 0)),
            pl.BlockSpec((tm, MLA_KV_RANK), lambda i: (i, 0)),
            pl.BlockSpec((tm, V7X_LANES), lambda i: (i, 0)),
        ],
        out_shape=[
            jax.ShapeDtypeStruct((m, MLA_Q_RANK), BF16),
            jax.ShapeDtypeStruct((m, MLA_KV_RANK), BF16),
            jax.ShapeDtypeStruct((m, V7X_LANES), BF16),
        ],
        compiler_params=_params("arbitrary"),
        name="mla_down",
    )(x, gain, w_down, qa, kva, kr, c, s1, s2)


def _mla_up_body(cq_ref, ckv_ref, kpe_ref, wuq_ref, wukv_ref, qn_ref, qr_ref, kn_ref, c_ref, s1_ref, s2_ref,
                 q_ref, k_ref, v_ref):
    scale = (MLA_NOPE + MLA_ROPE) ** -0.5
    q = jnp.dot(cq_ref[...], wuq_ref[...], preferred_element_type=F32)
    kv = jnp.dot(ckv_ref[...], wukv_ref[...], preferred_element_type=F32)
    kpe = kpe_ref[...]
    qng = qn_ref[...] * scale
    qrg = qr_ref[...] * scale
    kng = kn_ref[...]
    c, s1, s2 = c_ref[...], s1_ref[...], s2_ref[...]
    ones = jnp.ones(kpe.shape, BF16)
    for h in range(MLA_HEADS):
        lo = h * MLA_SLOT
        mid = lo + MLA_NOPE
        hi = lo + MLA_SLOT
        q_ref[0, h, :, 0:MLA_NOPE] = _rms(q[:, lo:mid], qng).astype(BF16)
        q_ref[0, h, :, MLA_NOPE:MLA_SLOT] = _rope64(_slot_rms(q[:, mid:hi], qrg), c, s1, s2).astype(BF16)
        k_ref[0, h, :, 0:MLA_NOPE] = _rms(kv[:, lo:mid], kng).astype(BF16)
        k_ref[0, h, :, MLA_NOPE:MLA_SLOT] = kpe
        v_ref[0, h, :, 0:MLA_V] = kv[:, mid:hi].astype(BF16)
        v_ref[0, h, :, MLA_V:MLA_SLOT] = ones


def _mla_up(cq, ckv, kpe, w_uq, w_ukv, qn, qr, kn, c, s1, s2, b, seq, tm):
    nt = seq // tm
    tok = lambda n: pl.BlockSpec((tm, n), lambda bi, i: (bi * nt + i, 0))
    full = lambda a: pl.BlockSpec(a.shape, lambda bi, i: (0, 0))
    tab = pl.BlockSpec((tm, V7X_LANES), lambda bi, i: (i, 0))
    out = pl.BlockSpec((1, MLA_HEADS, tm, MLA_SLOT), lambda bi, i: (bi, 0, i, 0))
    shape = jax.ShapeDtypeStruct((b, MLA_HEADS, seq, MLA_SLOT), BF16)
    return pl.pallas_call(
        _mla_up_body,
        grid=(b, nt),
        in_specs=[tok(MLA_Q_RANK), tok(MLA_KV_RANK), tok(V7X_LANES), full(w_uq), full(w_ukv),
                  full(qn), full(qr), full(kn), tab, tab, tab],
        out_specs=[out, out, out],
        out_shape=[shape, shape, shape],
        compiler_params=_params("arbitrary", "arbitrary"),
        name="mla_up",
    )(cq, ckv, kpe, w_uq, w_ukv, qn, qr, kn, c, s1, s2)


def _mla_attn_body(q_ref, k_ref, v_ref, o_ref):
    s = lax.dot_general(q_ref[0, 0], k_ref[0, 0], (((1,), (1,)), ((), ())), preferred_element_type=F32)
    m = jnp.max(s, axis=-1, keepdims=True)
    p = jnp.exp(s - m).astype(BF16)
    acc = jnp.dot(p, v_ref[0, 0], preferred_element_type=F32)
    o_ref[0] = (acc[:, :MLA_V] / acc[:, MLA_V:]).astype(BF16)


def _mla_attention(q, k, v, tq):
    b, nh, seq, slot = q.shape
    return pl.pallas_call(
        _mla_attn_body,
        grid=(b, nh, seq // tq),
        in_specs=[
            pl.BlockSpec((1, 1, tq, slot), lambda bi, h, i: (bi, h, i, 0)),
            pl.BlockSpec((1, 1, seq, slot), lambda bi, h, i: (bi, h, 0, 0)),
            pl.BlockSpec((1, 1, seq, slot), lambda bi, h, i: (bi, h, 0, 0)),
        ],
        out_specs=pl.BlockSpec((1, tq, MLA_V), lambda bi, h, i: (bi, i, h)),
        out_shape=jax.ShapeDtypeStruct((b, seq, nh * MLA_V), BF16),
        compiler_params=_params("arbitrary", "arbitrary", "arbitrary"),
        name="mla_attention",
    )(q, k, v)


def _swa_pair_perm(w_q_cols):
    lead = w_q_cols.shape[:-1]
    w = w_q_cols.reshape(lead + (2, 2, SWA_GROUP, SWA_HEAD_DIM))
    w = jnp.swapaxes(w, -3, -2)
    return w.reshape(lead + (SWA_HEADS * SWA_HEAD_DIM,))


def _rope_tables(positions):
    half = MLA_ROPE // 2
    inv = ROPE_THETA ** (-jnp.arange(0, MLA_ROPE, 2, dtype=F32) / MLA_ROPE)
    ang = positions.astype(F32)[:, None] * inv[None, :]
    cos, sin = jnp.cos(ang), jnp.sin(ang)
    z = jnp.zeros_like(cos)
    pad = jnp.zeros((positions.shape[0], V7X_LANES - MLA_ROPE), F32)
    c = jnp.concatenate([cos, cos, pad], axis=1)
    s1 = jnp.concatenate([-sin, z, pad], axis=1)
    s2 = jnp.concatenate([z, sin, pad], axis=1)
    return c, s1, s2


def _pad_lanes(v, n):
    return jnp.pad(v, ((0, 0), (0, n - v.shape[1])))


def kernel(x, positions, norm_mix_g, norm_ffn_g, pool_w, pool_scale, swa_w_qkv, swa_q_gain, swa_k_gain, swa_sinks, swa_w_o, mla_w_down, mla_q_a_gain, mla_kv_a_gain, mla_w_uq, mla_w_ukv, mla_qn_gain, mla_qr_gain, mla_kn_gain, mla_kr_gain, mla_w_o, ffn_w_in, ffn_conv_w, ffn_conv_b, ffn_w_out):
    b, seq, d = x.shape
    m = b * seq
    depth = norm_mix_g.shape[0]
    t = _tiles(m, seq)
    xf = x.reshape(m, d)

    w_in = ffn_w_in.astype(BF16)
    w_out = ffn_w_out.astype(BF16)
    f = ffn_w_out.shape[1]
    cwb = jnp.concatenate([ffn_conv_w, ffn_conv_b[:, None, :], jnp.zeros((depth, 4, f), F32)], axis=1)
    pool_wb = pool_w.astype(BF16)

    nq = SWA_HEADS * SWA_HEAD_DIM
    pos_col = positions.reshape(seq, 1)
    pos_row = positions.reshape(1, seq)
    blk = np.kron(np.eye(4, dtype=np.float32), np.ones((SWA_HEAD_DIM, SWA_HEAD_DIM), np.float32))
    e_blk = jnp.asarray(blk, dtype=BF16)
    c_tab, s1_tab, s2_tab = _rope_tables(positions)

    for i in range(depth):
        kind = i % N_MIXERS
        j = i // N_MIXERS
        gain = norm_mix_g[i][None, :]
        if kind == 0:
            xf = _pool_layer(xf, gain, pool_wb, pool_scale[j][None, :], j, seq, t["pool_ts"])
        elif kind == 1:
            w = swa_w_qkv[j]
            w_qkv = jnp.concatenate([_swa_pair_perm(w[:, :nq]), w[:, nq:]], axis=1).astype(BF16)
            w_o = _swa_pair_perm(swa_w_o[j].T).T.astype(BF16)
            qg = jnp.tile(swa_q_gain[j], 4)[None, :]
            kg = jnp.tile(swa_k_gain[j], 4)[None, :]
            q, k, v = _swa_qkv(xf, gain, w_qkv, qg, kg, e_blk, t["proj_tm"])
            o = _swa_attention(q.reshape(b, seq, -1), k.reshape(b, seq, -1), v.reshape(b, seq, -1),
                               pos_col, pos_row, swa_sinks[j], seq)
            xf = _proj_residual(o.reshape(m, -1), w_o, xf, t["res_tm"])
        else:
            n_lat = MLA_Q_RANK + MLA_KV_RANK
            w_down = _pad_lanes(mla_w_down[j], n_lat + V7X_LANES).astype(BF16)
            wq = mla_w_uq[j].reshape(MLA_Q_RANK, MLA_HEADS, MLA_NOPE + MLA_ROPE)
            wq = jnp.pad(wq, ((0, 0), (0, 0), (0, MLA_SLOT - MLA_NOPE - MLA_ROPE)))
            w_uq = wq.reshape(MLA_Q_RANK, MLA_HEADS * MLA_SLOT).astype(BF16)
            w_ukv = mla_w_ukv[j].astype(BF16)
            qr = _pad_lanes(mla_qr_gain[j][None, :], V7X_LANES)
            kr = _pad_lanes(mla_kr_gain[j][None, :], V7X_LANES)
            cq, ckv, kpe = _mla_down(xf, gain, w_down, mla_q_a_gain[j][None, :], mla_kv_a_gain[j][None, :], kr,
                                     c_tab, s1_tab, s2_tab, seq, t["proj_tm"])
            q, k, v = _mla_up(cq, ckv, kpe, w_uq, w_ukv, mla_qn_gain[j][None, :], qr, mla_kn_gain[j][None, :],
                              c_tab, s1_tab, s2_tab, b, seq, t["up_tm"])
            o = _mla_attention(q, k, v, t["attn_tq"])
            xf = _proj_residual(o.reshape(m, -1), mla_w_o[j].astype(BF16), xf, t["res_tm"])
        xf = _ffn_layer(xf, norm_ffn_g[i][None, :], w_in, cwb, w_out, i, seq, t["ffn_tm"], t["ffn_tf"])
    return xf.reshape(b, seq, d)
```

```python
import functools

import numpy as np
import jax
import jax.numpy as jnp
from jax import lax
from jax.experimental import pallas as pl
from jax.experimental.pallas import tpu as pltpu

F32 = jnp.float32
BF16 = jnp.bfloat16

EPS = 1e-6
LOG2E = 1.4426950408889634
N_MIXERS = 3
POOL_WINDOWS = (2, 4, 8, 16)
SWA_HEADS = 32
SWA_KV_HEADS = 4
SWA_GROUP = SWA_HEADS // SWA_KV_HEADS
SWA_HEAD_DIM = 64
SWA_WINDOW = 128
SWA_BLOCK = 128
SWA_SPAN = SWA_BLOCK + 2 * SWA_WINDOW
MLA_HEADS = 16
MLA_NOPE = 128
MLA_ROPE = 64
MLA_V = 128
MLA_Q_RANK = 512
MLA_KV_RANK = 512
MLA_SLOT = 256
MLA_VT_ROWS = MLA_V + 16
MLA_KEY_CHUNK = 512
ROPE_THETA = 10000.0

V7X_LANES = 128
V7X_SUBLANES = 8
V7X_VMEM_LIMIT_BYTES = 62 * 1024 * 1024


def _tiles(m, seq):
    def fit(t):
        while seq % t:
            t //= 2
        return t
    return dict(ffn_tm=fit(1024), ffn_tf=512, pool_ts=fit(256), proj_tm=fit(512), up_tm=fit(256),
                attn_tq=fit(512), res_tm=fit(512))


def _params(*sem):
    return pltpu.CompilerParams(dimension_semantics=sem, vmem_limit_bytes=V7X_VMEM_LIMIT_BYTES)


def _rms(x, g):
    ms = jnp.mean(x * x, axis=-1, keepdims=True)
    return x * lax.rsqrt(ms + EPS) * g


def _halo_specs(tm, m, d, nargs):
    r = tm // V7X_SUBLANES
    last = m // V7X_SUBLANES - 1
    if nargs == 1:
        prev = pl.BlockSpec((V7X_SUBLANES, d), lambda i: (jnp.maximum(i * r - 1, 0), 0))
        nxt = pl.BlockSpec((V7X_SUBLANES, d), lambda i: (jnp.minimum((i + 1) * r, last), 0))
    else:
        prev = pl.BlockSpec((V7X_SUBLANES, d), lambda i, j: (jnp.maximum(i * r - 1, 0), 0))
        nxt = pl.BlockSpec((V7X_SUBLANES, d), lambda i, j: (jnp.minimum((i + 1) * r, last), 0))
    return prev, nxt


def _ffn_body(xp_ref, x_ref, xn_ref, g_ref, wg_ref, wv_ref, cwb_ref, wo_ref, o_ref, hn_ref, *, tm, seq):
    i = pl.program_id(0)
    j = pl.program_id(1)

    @pl.when(j == 0)
    def _():
        g = g_ref[...]
        x = x_ref[...]
        hn_ref[0:tm, :] = _rms(x, g).astype(BF16)
        s0 = (i * tm) % seq
        keep_prev = jnp.where(s0 != 0, 7, -1)
        keep_next = jnp.where(s0 + tm != seq, 8, -1)
        ext = jnp.concatenate([_rms(xp_ref[...], g), _rms(xn_ref[...], g)], axis=0)
        r = lax.broadcasted_iota(jnp.int32, (16, 1), 0)
        ext = jnp.where((r == keep_prev) | (r == keep_next), ext, 0.0)
        hn_ref[tm:tm + 16, :] = ext.astype(BF16)
        o_ref[...] = x

    ug = jnp.dot(hn_ref[...], wg_ref[...], preferred_element_type=F32)
    uv = jnp.dot(hn_ref[0:tm, :], wv_ref[...], preferred_element_type=F32)
    gm = ug[0:tm]
    g_prev = ug[tm + 7:tm + 8]
    g_next = ug[tm + 8:tm + 9]
    rows = lax.broadcasted_iota(jnp.int32, (tm, 1), 0)
    up = jnp.where(rows == 0, g_prev, pltpu.roll(gm, 1, 0))
    dn = jnp.where(rows == tm - 1, g_next, pltpu.roll(gm, tm - 1, 0))
    cwb = cwb_ref[...]
    gc = up * cwb[0:1] + gm * cwb[1:2] + dn * cwb[2:3] + cwb[3:4]
    act = (gc * jax.nn.sigmoid(gc) * uv).astype(BF16)
    o_ref[...] += jnp.dot(act, wo_ref[...], preferred_element_type=F32)


def _ffn_layer(x, gain, w_in, cwb, w_out, layer, seq, tm, tf):
    m, d = x.shape
    f = w_out.shape[1]
    nf = f // tf
    prev, nxt = _halo_specs(tm, m, d, 2)
    return pl.pallas_call(
        functools.partial(_ffn_body, tm=tm, seq=seq),
        grid=(m // tm, nf),
        in_specs=[
            prev,
            pl.BlockSpec((tm, d), lambda i, j: (i, 0)),
            nxt,
            pl.BlockSpec((1, d), lambda i, j: (0, 0)),
            pl.BlockSpec((None, d, tf), lambda i, j: (layer, 0, j)),
            pl.BlockSpec((None, d, tf), lambda i, j: (layer, 0, j + nf)),
            pl.BlockSpec((None, 8, tf), lambda i, j: (layer, 0, j)),
            pl.BlockSpec((None, tf, d), lambda i, j: (layer, j, 0)),
        ],
        out_specs=pl.BlockSpec((tm, d), lambda i, j: (i, 0)),
        out_shape=jax.ShapeDtypeStruct((m, d), F32),
        scratch_shapes=[pltpu.VMEM((tm + 16, d), BF16)],
        compiler_params=_params("arbitrary", "arbitrary"),
        name="ffn_convglu",
    )(x, x, x, gain, w_in, w_in, cwb, w_out)


def _pool_body(xp_ref, x_ref, xn_ref, g_ref, w_ref, sc_ref, o_ref, *, ts, seq):
    i = pl.program_id(0)
    g = g_ref[...]
    x = x_ref[...]
    s0 = (i * ts) % seq
    h = _rms(x, g)
    hp = _rms(xp_ref[...], g) * jnp.where(s0 != 0, 1.0, 0.0)
    hx = _rms(xn_ref[...], g) * jnp.where(s0 + ts != seq, 1.0, 0.0)
    n = ts + 16
    hext = jnp.concatenate([hp, h, hx], axis=0)
    pos = s0 + lax.broadcasted_iota(jnp.int32, (ts, 1), 0)
    dg = x.shape[1] // len(POOL_WINDOWS)
    for gi, w in enumerate(POOL_WINDOWS):
        left = w // 2
        right = w - 1 - left
        cols = slice(gi * dg, (gi + 1) * dg)
        p = hext[:, cols]
        k = 1
        while k < w:
            p = p + pltpu.roll(p, n - k, 0)
            k *= 2
        wsum = pltpu.roll(p, left, 0)[8:8 + ts]
        cnt = (jnp.minimum(pos + right + 1, seq) - jnp.maximum(pos - left, 0)).astype(F32)
        pooled = (wsum / cnt - h[:, cols]).astype(BF16)
        y = jnp.dot(pooled, w_ref[gi], preferred_element_type=F32) * sc_ref[:, cols]
        o_ref[:, cols] = x[:, cols] + y


def _pool_layer(x, gain, pool_w, scale, layer, seq, ts):
    m, d = x.shape
    ng, dg = pool_w.shape[1], pool_w.shape[2]
    prev, nxt = _halo_specs(ts, m, d, 1)
    return pl.pallas_call(
        functools.partial(_pool_body, ts=ts, seq=seq),
        grid=(m // ts,),
        in_specs=[
            prev,
            pl.BlockSpec((ts, d), lambda i: (i, 0)),
            nxt,
            pl.BlockSpec((1, d), lambda i: (0, 0)),
            pl.BlockSpec((None, ng, dg, dg), lambda i: (layer, 0, 0, 0)),
            pl.BlockSpec((1, d), lambda i: (0, 0)),
        ],
        out_specs=pl.BlockSpec((ts, d), lambda i: (i, 0)),
        out_shape=jax.ShapeDtypeStruct((m, d), F32),
        compiler_params=_params("arbitrary"),
        name="pool_mixer",
    )(x, x, x, gain, pool_w, scale)


def _proj_res_body(a_ref, w_ref, x_ref, o_ref):
    o_ref[...] = x_ref[...] + jnp.dot(a_ref[...], w_ref[...], preferred_element_type=F32)


def _proj_residual(a, w, x, tm):
    m, d = x.shape
    k = a.shape[1]
    return pl.pallas_call(
        _proj_res_body,
        grid=(m // tm,),
        in_specs=[
            pl.BlockSpec((tm, k), lambda i: (i, 0)),
            pl.BlockSpec((k, d), lambda i: (0, 0)),
            pl.BlockSpec((tm, d), lambda i: (i, 0)),
        ],
        out_specs=pl.BlockSpec((tm, d), lambda i: (i, 0)),
        out_shape=jax.ShapeDtypeStruct((m, d), F32),
        compiler_params=_params("arbitrary"),
        name="proj_residual",
    )(a, w, x)


def _group_sumsq(z, e):
    s = z * z
    s_hi = s.astype(BF16)
    s_lo = (s - s_hi.astype(F32)).astype(BF16)
    return (jnp.dot(s_hi, e, preferred_element_type=F32) + jnp.dot(s_lo, e, preferred_element_type=F32))


def _swa_qkv_body(x_ref, g_ref, w_ref, qg_ref, kg_ref, e_ref, q_ref, k_ref, v_ref):
    hn = _rms(x_ref[...], g_ref[...]).astype(BF16)
    qkv = jnp.dot(hn, w_ref[...], preferred_element_type=F32)
    e = e_ref[...]
    nq = q_ref.shape[1]
    nkv = k_ref.shape[1]
    cw = e.shape[0]
    inv_hd = 1.0 / SWA_HEAD_DIM

    def head_norm(z, gain):
        return z * lax.rsqrt(_group_sumsq(z, e) * inv_hd + EPS) * gain

    qg = qg_ref[...] * (SWA_HEAD_DIM ** -0.5 * LOG2E)
    for c in range(nq // cw):
        cols = slice(c * cw, (c + 1) * cw)
        q_ref[:, cols] = head_norm(qkv[:, cols], qg).astype(BF16)
    k_ref[...] = head_norm(qkv[:, nq:nq + nkv], kg_ref[...]).astype(BF16)
    v_ref[...] = qkv[:, nq + nkv:].astype(BF16)


def _swa_qkv(x, gain, w_qkv, qg, kg, e, tm):
    m, d = x.shape
    nkv = SWA_KV_HEADS * SWA_HEAD_DIM
    nq = w_qkv.shape[1] - 2 * nkv
    return pl.pallas_call(
        _swa_qkv_body,
        grid=(m // tm,),
        in_specs=[
            pl.BlockSpec((tm, d), lambda i: (i, 0)),
            pl.BlockSpec((1, d), lambda i: (0, 0)),
            pl.BlockSpec(w_qkv.shape, lambda i: (0, 0)),
            pl.BlockSpec((1, nkv), lambda i: (0, 0)),
            pl.BlockSpec((1, nkv), lambda i: (0, 0)),
            pl.BlockSpec(e.shape, lambda i: (0, 0)),
        ],
        out_specs=[
            pl.BlockSpec((tm, nq), lambda i: (i, 0)),
            pl.BlockSpec((tm, nkv), lambda i: (i, 0)),
            pl.BlockSpec((tm, nkv), lambda i: (i, 0)),
        ],
        out_shape=[
            jax.ShapeDtypeStruct((m, nq), BF16),
            jax.ShapeDtypeStruct((m, nkv), BF16),
            jax.ShapeDtypeStruct((m, nkv), BF16),
        ],
        compiler_params=_params("arbitrary"),
        name="swa_qkv",
    )(x, gain, w_qkv, qg, kg, e)


def _swa_slopes():
    return [float(2.0 ** (-8.0 * (i + 1) / SWA_HEADS)) for i in range(SWA_HEADS)]


def _swa_attn_body(sink_ref, q_ref, k_ref, v_ref, pc_ref, pr_ref, o_ref, *, seq):
    j = pl.program_id(1)
    hd = SWA_HEAD_DIM
    start = pl.multiple_of(jnp.clip(j * SWA_BLOCK - SWA_WINDOW, 0, seq - SWA_SPAN), SWA_BLOCK)
    kblk = k_ref[0, pl.ds(start, SWA_SPAN), :]
    vblk = v_ref[0, pl.ds(start, SWA_SPAN), :]
    pk = pr_ref[:, pl.ds(start, SWA_SPAN)]
    pq = pc_ref[...]
    qi = j * SWA_BLOCK + lax.broadcasted_iota(jnp.int32, (SWA_BLOCK, 1), 0)
    ki = start + lax.broadcasted_iota(jnp.int32, (1, SWA_SPAN), 1)
    in_win = jnp.abs(qi - ki) <= SWA_WINDOW
    dist = jnp.where(in_win, jnp.abs(pq - pk).astype(F32), jnp.inf)
    lane = lax.broadcasted_iota(jnp.int32, (1, 2 * hd), 1)
    slopes = _swa_slopes()
    zero = jnp.zeros((), BF16)
    lane_full = lax.broadcasted_iota(jnp.int32, (SWA_SPAN, 2 * hd), 1)
    ind_a = jnp.where(lane_full < hd, 1.0, 0.0).astype(BF16)
    ind_b = jnp.where(lane_full >= hd, 1.0, 0.0).astype(BF16)
    for mp in range(SWA_KV_HEADS // 2):
        kp = kblk[:, mp * 2 * hd:(mp + 1) * 2 * hd]
        vp = vblk[:, mp * 2 * hd:(mp + 1) * 2 * hd]
        k2 = jnp.concatenate([jnp.where(lane < hd, kp, zero), jnp.where(lane >= hd, kp, zero)], axis=0)
        v2 = jnp.concatenate([jnp.concatenate([jnp.where(lane < hd, vp, zero), ind_a], axis=1),
                              jnp.concatenate([jnp.where(lane >= hd, vp, zero), ind_b], axis=1)], axis=0)
        pair_cols = [slice((mp * SWA_GROUP + g) * 2 * hd, (mp * SWA_GROUP + g + 1) * 2 * hd)
                     for g in range(SWA_GROUP)]
        q_all = jnp.concatenate([q_ref[0, :, cols] for cols in pair_cols], axis=0)
        s_all = lax.dot_general(q_all, k2, (((1,), (1,)), ((), ())), preferred_element_type=F32)
        ps, sink_terms = [], []
        for g in range(SWA_GROUP):
            ha = (2 * mp) * SWA_GROUP + g
            hb = (2 * mp + 1) * SWA_GROUP + g
            s = s_all[g * SWA_BLOCK:(g + 1) * SWA_BLOCK]
            sa = s[:, :SWA_SPAN] - (LOG2E * slopes[ha]) * dist
            sb = s[:, SWA_SPAN:] - (LOG2E * slopes[hb]) * dist
            sink_a = sink_ref[ha] * LOG2E
            sink_b = sink_ref[hb] * LOG2E
            ma = jnp.maximum(jnp.max(sa, axis=-1, keepdims=True), sink_a)
            mb = jnp.maximum(jnp.max(sb, axis=-1, keepdims=True), sink_b)
            ps.append(jnp.concatenate([jnp.exp2(sa - ma), jnp.exp2(sb - mb)], axis=1).astype(BF16))
            sink_terms.append(jnp.where(lane < hd, jnp.exp2(sink_a - ma), jnp.exp2(sink_b - mb)))
        o_all = jnp.dot(jnp.concatenate(ps, axis=0), v2, preferred_element_type=F32)
        for g in range(SWA_GROUP):
            og = o_all[g * SWA_BLOCK:(g + 1) * SWA_BLOCK]
            o_ref[0, :, pair_cols[g]] = (og[:, :2 * hd] / (og[:, 2 * hd:] + sink_terms[g])).astype(BF16)


def _swa_attention(q, k, v, pos_col, pos_row, sinks, seq):
    b = q.shape[0]
    nq = q.shape[2]
    nkv = k.shape[2]
    return pl.pallas_call(
        functools.partial(_swa_attn_body, seq=seq),
        grid=(b, seq // SWA_BLOCK),
        in_specs=[
            pl.BlockSpec(memory_space=pltpu.SMEM),
            pl.BlockSpec((1, SWA_BLOCK, nq), lambda bi, j: (bi, j, 0)),
            pl.BlockSpec((1, seq, nkv), lambda bi, j: (bi, 0, 0)),
            pl.BlockSpec((1, seq, nkv), lambda bi, j: (bi, 0, 0)),
            pl.BlockSpec((SWA_BLOCK, 1), lambda bi, j: (j, 0)),
            pl.BlockSpec((1, seq), lambda bi, j: (0, 0)),
        ],
        out_specs=pl.BlockSpec((1, SWA_BLOCK, nq), lambda bi, j: (bi, j, 0)),
        out_shape=jax.ShapeDtypeStruct(q.shape, BF16),
        compiler_params=_params("arbitrary", "arbitrary"),
        name="swa_attention",
    )(sinks, q, k, v, pos_col, pos_row)


def _rope64(z, c, s1, s2):
    return z * c + pltpu.roll(z, V7X_LANES - MLA_ROPE // 2, 1) * s1 + pltpu.roll(z, MLA_ROPE // 2, 1) * s2


def _slot_rms(z, gain):
    ms = jnp.sum(z * z, axis=-1, keepdims=True) * (1.0 / MLA_ROPE)
    return z * lax.rsqrt(ms + EPS) * gain


def _mla_down_body(x_ref, g_ref, w_ref, qa_ref, kva_ref, kr_ref, c_ref, s1_ref, s2_ref,
                   cq_ref, ckv_ref, kpe_ref):
    hn = _rms(x_ref[...], g_ref[...]).astype(BF16)
    d = jnp.dot(hn, w_ref[...], preferred_element_type=F32)
    cq_ref[...] = _rms(d[:, :MLA_Q_RANK], qa_ref[...]).astype(BF16)
    ckv_ref[...] = _rms(d[:, MLA_Q_RANK:MLA_Q_RANK + MLA_KV_RANK], kva_ref[...]).astype(BF16)
    kp = _slot_rms(d[:, MLA_Q_RANK + MLA_KV_RANK:], kr_ref[...])
    kpe_ref[...] = _rope64(kp, c_ref[...], s1_ref[...], s2_ref[...]).astype(BF16)


def _mla_down(x, gain, w_down, qa, kva, kr, c, s1, s2, seq, tm):
    m, d = x.shape
    nt = seq // tm
    tab = pl.BlockSpec((tm, V7X_LANES), lambda i: (i % nt, 0))
    row = lambda n: pl.BlockSpec((1, n), lambda i: (0, 0))
    return pl.pallas_call(
        _mla_down_body,
        grid=(m // tm,),
        in_specs=[
            pl.BlockSpec((tm, d), lambda i: (i, 0)),
            row(d),
            pl.BlockSpec(w_down.shape, lambda i: (0, 0)),
            row(MLA_Q_RANK), row(MLA_KV_RANK), row(V7X_LANES),
            tab, tab, tab,
        ],
        out_specs=[
            pl.BlockSpec((tm, MLA_Q_RANK), lambda i: (i, 0)),
            pl.BlockSpec((tm, MLA_KV_RANK), lambda i: (i, 0)),
            pl.BlockSpec((tm, V7X_LANES), lambda i: (i, 0)),
        ],
        out_shape=[
            jax.ShapeDtypeStruct((m, MLA_Q_RANK), BF16),
            jax.ShapeDtypeStruct((m, MLA_KV_RANK), BF16),
            jax.ShapeDtypeStruct((m, V7X_LANES), BF16),
        ],
        compiler_params=_params("arbitrary"),
        name="mla_down",
    )(x, gain, w_down, qa, kva, kr, c, s1, s2)


def _mla_up_body(cq_ref, ckv_ref, kpe_ref, wuq_ref, wuk_ref, wuvt_ref, qn_ref, qr_ref, kn_ref,
                 c_ref, s1_ref, s2_ref, q_ref, k_ref, vt_ref):
    scale = (MLA_NOPE + MLA_ROPE) ** -0.5 * LOG2E
    ckv = ckv_ref[...]
    q = jnp.dot(cq_ref[...], wuq_ref[...], preferred_element_type=F32)
    kn = jnp.dot(ckv, wuk_ref[...], preferred_element_type=F32)
    vt = lax.dot_general(wuvt_ref[...], ckv, (((1,), (1,)), ((), ())), preferred_element_type=F32)
    kpe = kpe_ref[...]
    qng = qn_ref[...] * scale
    qrg = qr_ref[...] * scale
    kng = kn_ref[...]
    c, s1, s2 = c_ref[...], s1_ref[...], s2_ref[...]
    ones = jnp.ones((MLA_VT_ROWS - MLA_V, ckv.shape[0]), BF16)
    for h in range(MLA_HEADS):
        lo = h * MLA_SLOT
        mid = lo + MLA_NOPE
        hi = lo + MLA_SLOT
        q_ref[0, h, :, 0:MLA_NOPE] = _rms(q[:, lo:mid], qng).astype(BF16)
        q_ref[0, h, :, MLA_NOPE:MLA_SLOT] = _rope64(_slot_rms(q[:, mid:hi], qrg), c, s1, s2).astype(BF16)
        k_ref[0, h, :, 0:MLA_NOPE] = _rms(kn[:, h * MLA_NOPE:(h + 1) * MLA_NOPE], kng).astype(BF16)
        k_ref[0, h, :, MLA_NOPE:MLA_SLOT] = kpe
        vt_ref[0, h, 0:MLA_V, :] = vt[h * MLA_V:(h + 1) * MLA_V, :].astype(BF16)
        vt_ref[0, h, MLA_V:MLA_VT_ROWS, :] = ones


def _mla_up(cq, ckv, kpe, w_uq, w_uk, w_uvt, qn, qr, kn, c, s1, s2, b, seq, tm):
    nt = seq // tm
    tok = lambda n: pl.BlockSpec((tm, n), lambda bi, i: (bi * nt + i, 0))
    full = lambda a: pl.BlockSpec(a.shape, lambda bi, i: (0, 0))
    tab = pl.BlockSpec((tm, V7X_LANES), lambda bi, i: (i, 0))
    out = pl.BlockSpec((1, MLA_HEADS, tm, MLA_SLOT), lambda bi, i: (bi, 0, i, 0))
    shape = jax.ShapeDtypeStruct((b, MLA_HEADS, seq, MLA_SLOT), BF16)
    out_t = pl.BlockSpec((1, MLA_HEADS, MLA_VT_ROWS, tm), lambda bi, i: (bi, 0, 0, i))
    shape_t = jax.ShapeDtypeStruct((b, MLA_HEADS, MLA_VT_ROWS, seq), BF16)
    return pl.pallas_call(
        _mla_up_body,
        grid=(b, nt),
        in_specs=[tok(MLA_Q_RANK), tok(MLA_KV_RANK), tok(V7X_LANES), full(w_uq), full(w_uk), full(w_uvt),
                  full(qn), full(qr), full(kn), tab, tab, tab],
        out_specs=[out, out, out_t],
        out_shape=[shape, shape, shape_t],
        compiler_params=_params("arbitrary", "arbitrary"),
        name="mla_up",
    )(cq, ckv, kpe, w_uq, w_uk, w_uvt, qn, qr, kn, c, s1, s2)


def _mla_attn_body(q_ref, k_ref, vt_ref, o_ref, s0_ref, s1_ref, m0_ref, m1_ref):
    step = pl.program_id(0)

    @pl.when(step == 0)
    def _():
        s1_ref[...] = jnp.zeros(s1_ref.shape, F32)
        m1_ref[...] = jnp.zeros(m1_ref.shape, F32)

    def work(s_new, m_new, s_old, m_old):
        q = q_ref[0, 0]
        m_prev = m_old[...]
        nc = s_new.shape[0] // MLA_KEY_CHUNK
        chunk = lambda c: slice(c * MLA_KEY_CHUNK, (c + 1) * MLA_KEY_CHUNK)

        def probs(c):
            return jnp.exp2(s_old[chunk(c), :] - m_prev).astype(BF16)

        def weigh(c, p, ot):
            part = jnp.dot(vt_ref[0, 0, :, chunk(c)], p, preferred_element_type=F32)
            return part if ot is None else ot + part

        def score(c, m_run):
            st = lax.dot_general(k_ref[0, 0, chunk(c), :], q, (((1,), (1,)), ((), ())),
                                 preferred_element_type=F32)
            s_new[chunk(c), :] = st
            mc = jnp.max(st, axis=0, keepdims=True)
            return mc if m_run is None else jnp.maximum(m_run, mc)

        ot = weigh(0, probs(0), None)
        m_run = None
        for c in range(nc - 1):
            m_run = score(c, m_run)
            ot = weigh(c + 1, probs(c + 1), ot)
        o = ot[0:MLA_V] / ot[MLA_V:MLA_V + 1]
        o_ref[0] = o.T.astype(BF16)
        m_new[...] = score(nc - 1, m_run)

    @pl.when(step % 2 == 0)
    def _():
        work(s0_ref, m0_ref, s1_ref, m1_ref)

    @pl.when(step % 2 == 1)
    def _():
        work(s1_ref, m1_ref, s0_ref, m0_ref)


def _mla_attention(q, k, vt, tq):
    b, nh, seq, slot = q.shape
    nq = seq // tq
    n = b * nh * nq

    def tile(t):
        return t // (nh * nq), (t // nq) % nh, t % nq

    def q_map(s):
        bi, h, i = tile(jnp.minimum(s, n - 1))
        return bi, h, i, 0

    def k_map(s):
        bi, h, _ = tile(jnp.minimum(s, n - 1))
        return bi, h, 0, 0

    def vt_map(s):
        bi, h, _ = tile(jnp.maximum(s - 1, 0))
        return bi, h, 0, 0

    def o_map(s):
        bi, h, i = tile(jnp.maximum(s - 1, 0))
        return bi, i, h

    return pl.pallas_call(
        _mla_attn_body,
        grid=(n + 1,),
        in_specs=[
            pl.BlockSpec((1, 1, tq, slot), q_map),
            pl.BlockSpec((1, 1, seq, slot), k_map),
            pl.BlockSpec((1, 1, MLA_VT_ROWS, seq), vt_map),
        ],
        out_specs=pl.BlockSpec((1, tq, MLA_V), o_map),
        out_shape=jax.ShapeDtypeStruct((b, seq, nh * MLA_V), BF16),
        scratch_shapes=[pltpu.VMEM((seq, tq), F32), pltpu.VMEM((seq, tq), F32),
                        pltpu.VMEM((1, tq), F32), pltpu.VMEM((1, tq), F32)],
        compiler_params=_params("arbitrary"),
        name="mla_attention",
    )(q, k, vt)


def _swa_pair_perm(w_q_cols):
    lead = w_q_cols.shape[:-1]
    w = w_q_cols.reshape(lead + (2, 2, SWA_GROUP, SWA_HEAD_DIM))
    w = jnp.swapaxes(w, -3, -2)
    return w.reshape(lead + (SWA_HEADS * SWA_HEAD_DIM,))


def _rope_tables(positions):
    half = MLA_ROPE // 2
    inv = ROPE_THETA ** (-jnp.arange(0, MLA_ROPE, 2, dtype=F32) / MLA_ROPE)
    ang = positions.astype(F32)[:, None] * inv[None, :]
    cos, sin = jnp.cos(ang), jnp.sin(ang)
    z = jnp.zeros_like(cos)
    pad = jnp.zeros((positions.shape[0], V7X_LANES - MLA_ROPE), F32)
    c = jnp.concatenate([cos, cos, pad], axis=1)
    s1 = jnp.concatenate([-sin, z, pad], axis=1)
    s2 = jnp.concatenate([z, sin, pad], axis=1)
    return c, s1, s2


def _pad_lanes(v, n):
    return jnp.pad(v, ((0, 0), (0, n - v.shape[1])))


def kernel(x, positions, norm_mix_g, norm_ffn_g, pool_w, pool_scale, swa_w_qkv, swa_q_gain, swa_k_gain, swa_sinks, swa_w_o, mla_w_down, mla_q_a_gain, mla_kv_a_gain, mla_w_uq, mla_w_ukv, mla_qn_gain, mla_qr_gain, mla_kn_gain, mla_kr_gain, mla_w_o, ffn_w_in, ffn_conv_w, ffn_conv_b, ffn_w_out):
    b, seq, d = x.shape
    m = b * seq
    depth = norm_mix_g.shape[0]
    t = _tiles(m, seq)
    xf = x.reshape(m, d)

    w_in = ffn_w_in.astype(BF16)
    w_out = ffn_w_out.astype(BF16)
    f = ffn_w_out.shape[1]
    cwb = jnp.concatenate([ffn_conv_w, ffn_conv_b[:, None, :], jnp.zeros((depth, 4, f), F32)], axis=1)
    pool_wb = pool_w.astype(BF16)

    nq = SWA_HEADS * SWA_HEAD_DIM
    pos_col = positions.reshape(seq, 1)
    pos_row = positions.reshape(1, seq)
    blk = np.kron(np.eye(4, dtype=np.float32), np.ones((SWA_HEAD_DIM, SWA_HEAD_DIM), np.float32))
    e_blk = jnp.asarray(blk, dtype=BF16)
    c_tab, s1_tab, s2_tab = _rope_tables(positions)

    for i in range(depth):
        kind = i % N_MIXERS
        j = i // N_MIXERS
        gain = norm_mix_g[i][None, :]
        if kind == 0:
            xf = _pool_layer(xf, gain, pool_wb, pool_scale[j][None, :], j, seq, t["pool_ts"])
        elif kind == 1:
            w = swa_w_qkv[j]
            w_qkv = jnp.concatenate([_swa_pair_perm(w[:, :nq]), w[:, nq:]], axis=1).astype(BF16)
            w_o = _swa_pair_perm(swa_w_o[j].T).T.astype(BF16)
            qg = jnp.tile(swa_q_gain[j], 4)[None, :]
            kg = jnp.tile(swa_k_gain[j], 4)[None, :]
            q, k, v = _swa_qkv(xf, gain, w_qkv, qg, kg, e_blk, t["proj_tm"])
            o = _swa_attention(q.reshape(b, seq, -1), k.reshape(b, seq, -1), v.reshape(b, seq, -1),
                               pos_col, pos_row, swa_sinks[j], seq)
            xf = _proj_residual(o.reshape(m, -1), w_o, xf, t["res_tm"])
        else:
            n_lat = MLA_Q_RANK + MLA_KV_RANK
            w_down = _pad_lanes(mla_w_down[j], n_lat + V7X_LANES).astype(BF16)
            wq = mla_w_uq[j].reshape(MLA_Q_RANK, MLA_HEADS, MLA_NOPE + MLA_ROPE)
            wq = jnp.pad(wq, ((0, 0), (0, 0), (0, MLA_SLOT - MLA_NOPE - MLA_ROPE)))
            w_uq = wq.reshape(MLA_Q_RANK, MLA_HEADS * MLA_SLOT).astype(BF16)
            wkv = mla_w_ukv[j].reshape(MLA_KV_RANK, MLA_HEADS, MLA_NOPE + MLA_V)
            w_uk = wkv[:, :, :MLA_NOPE].reshape(MLA_KV_RANK, MLA_HEADS * MLA_NOPE).astype(BF16)
            w_uvt = wkv[:, :, MLA_NOPE:].reshape(MLA_KV_RANK, MLA_HEADS * MLA_V).T.astype(BF16)
            qr = _pad_lanes(mla_qr_gain[j][None, :], V7X_LANES)
            kr = _pad_lanes(mla_kr_gain[j][None, :], V7X_LANES)
            cq, ckv, kpe = _mla_down(xf, gain, w_down, mla_q_a_gain[j][None, :], mla_kv_a_gain[j][None, :], kr,
                                     c_tab, s1_tab, s2_tab, seq, t["proj_tm"])
            q, k, vt = _mla_up(cq, ckv, kpe, w_uq, w_uk, w_uvt, mla_qn_gain[j][None, :], qr,
                               mla_kn_gain[j][None, :], c_tab, s1_tab, s2_tab, b, seq, t["up_tm"])
            o = _mla_attention(q, k, vt, t["attn_tq"])
            xf = _proj_residual(o.reshape(m, -1), mla_w_o[j].astype(BF16), xf, t["res_tm"])
        xf = _ffn_layer(xf, norm_ffn_g[i][None, :], w_in, cwb, w_out, i, seq, t["ffn_tm"], t["ffn_tf"])
    return xf.reshape(b, seq, d)
```

```python
import functools

import numpy as np
import jax
import jax.numpy as jnp
from jax import lax
from jax.experimental import pallas as pl
from jax.experimental.pallas import tpu as pltpu

F32 = jnp.float32
BF16 = jnp.bfloat16

EPS = 1e-6
LOG2E = 1.4426950408889634
N_MIXERS = 3
POOL_WINDOWS = (2, 4, 8, 16)
SWA_HEADS = 32
SWA_KV_HEADS = 4
SWA_GROUP = SWA_HEADS // SWA_KV_HEADS
SWA_HEAD_DIM = 64
SWA_WINDOW = 128
SWA_BLOCK = 128
SWA_SPAN = SWA_BLOCK + 2 * SWA_WINDOW
MLA_HEADS = 16
MLA_NOPE = 128
MLA_ROPE = 64
MLA_V = 128
MLA_Q_RANK = 512
MLA_KV_RANK = 512
MLA_SLOT = 256
MLA_VT_ROWS = MLA_V + 16
MLA_KEY_CHUNK = 512
ROPE_THETA = 10000.0

V7X_LANES = 128
V7X_SUBLANES = 8
V7X_VMEM_LIMIT_BYTES = 62 * 1024 * 1024


def _tiles(m, seq):
    def fit(t):
        while seq % t:
            t //= 2
        return t
    return dict(ffn_tm=fit(1024), ffn_tf=512, pool_ts=fit(256), proj_tm=fit(512), up_tm=fit(256),
                attn_tq=fit(512), res_tm=fit(512))


def _params(*sem):
    return pltpu.CompilerParams(dimension_semantics=sem, vmem_limit_bytes=V7X_VMEM_LIMIT_BYTES)


def _rms(x, g):
    ms = jnp.mean(x * x, axis=-1, keepdims=True)
    return x * lax.rsqrt(ms + EPS) * g


def _halo_specs(tm, m, d, nargs):
    r = tm // V7X_SUBLANES
    last = m // V7X_SUBLANES - 1
    if nargs == 1:
        prev = pl.BlockSpec((V7X_SUBLANES, d), lambda i: (jnp.maximum(i * r - 1, 0), 0))
        nxt = pl.BlockSpec((V7X_SUBLANES, d), lambda i: (jnp.minimum((i + 1) * r, last), 0))
    else:
        prev = pl.BlockSpec((V7X_SUBLANES, d), lambda i, j: (jnp.maximum(i * r - 1, 0), 0))
        nxt = pl.BlockSpec((V7X_SUBLANES, d), lambda i, j: (jnp.minimum((i + 1) * r, last), 0))
    return prev, nxt


def _ffn_body(xp_ref, x_ref, xn_ref, g_ref, wg_ref, wv_ref, cwb_ref, wo_ref, o_ref, hn_ref, *, tm, seq):
    i = pl.program_id(0)
    j = pl.program_id(1)

    @pl.when(j == 0)
    def _():
        g = g_ref[...]
        x = x_ref[...]
        hn_ref[0:tm, :] = _rms(x, g).astype(BF16)
        s0 = (i * tm) % seq
        keep_prev = jnp.where(s0 != 0, 7, -1)
        keep_next = jnp.where(s0 + tm != seq, 8, -1)
        ext = jnp.concatenate([_rms(xp_ref[...], g), _rms(xn_ref[...], g)], axis=0)
        r = lax.broadcasted_iota(jnp.int32, (16, 1), 0)
        ext = jnp.where((r == keep_prev) | (r == keep_next), ext, 0.0)
        hn_ref[tm:tm + 16, :] = ext.astype(BF16)
        o_ref[...] = x

    ug = jnp.dot(hn_ref[...], wg_ref[...], preferred_element_type=F32)
    uv = jnp.dot(hn_ref[0:tm, :], wv_ref[...], preferred_element_type=F32)
    gm = ug[0:tm]
    g_prev = ug[tm + 7:tm + 8]
    g_next = ug[tm + 8:tm + 9]
    rows = lax.broadcasted_iota(jnp.int32, (tm, 1), 0)
    up = jnp.where(rows == 0, g_prev, pltpu.roll(gm, 1, 0))
    dn = jnp.where(rows == tm - 1, g_next, pltpu.roll(gm, tm - 1, 0))
    cwb = cwb_ref[...]
    gc = up * cwb[0:1] + gm * cwb[1:2] + dn * cwb[2:3] + cwb[3:4]
    act = (gc * jax.nn.sigmoid(gc) * uv).astype(BF16)
    o_ref[...] += jnp.dot(act, wo_ref[...], preferred_element_type=F32)


def _ffn_layer(x, gain, w_in, cwb, w_out, layer, seq, tm, tf):
    m, d = x.shape
    f = w_out.shape[1]
    nf = f // tf
    prev, nxt = _halo_specs(tm, m, d, 2)
    return pl.pallas_call(
        functools.partial(_ffn_body, tm=tm, seq=seq),
        grid=(m // tm, nf),
        in_specs=[
            prev,
            pl.BlockSpec((tm, d), lambda i, j: (i, 0)),
            nxt,
            pl.BlockSpec((1, d), lambda i, j: (0, 0)),
            pl.BlockSpec((None, d, tf), lambda i, j: (layer, 0, j)),
            pl.BlockSpec((None, d, tf), lambda i, j: (layer, 0, j + nf)),
            pl.BlockSpec((None, 8, tf), lambda i, j: (layer, 0, j)),
            pl.BlockSpec((None, tf, d), lambda i, j: (layer, j, 0)),
        ],
        out_specs=pl.BlockSpec((tm, d), lambda i, j: (i, 0)),
        out_shape=jax.ShapeDtypeStruct((m, d), F32),
        scratch_shapes=[pltpu.VMEM((tm + 16, d), BF16)],
        compiler_params=_params("arbitrary", "arbitrary"),
        name="ffn_convglu",
    )(x, x, x, gain, w_in, w_in, cwb, w_out)


def _pool_body(xp_ref, x_ref, xn_ref, g_ref, w_ref, sc_ref, o_ref, *, ts, seq):
    i = pl.program_id(0)
    g = g_ref[...]
    x = x_ref[...]
    s0 = (i * ts) % seq
    h = _rms(x, g)
    hp = _rms(xp_ref[...], g) * jnp.where(s0 != 0, 1.0, 0.0)
    hx = _rms(xn_ref[...], g) * jnp.where(s0 + ts != seq, 1.0, 0.0)
    n = ts + 16
    hext = jnp.concatenate([hp, h, hx], axis=0)
    pos = s0 + lax.broadcasted_iota(jnp.int32, (ts, 1), 0)
    dg = x.shape[1] // len(POOL_WINDOWS)
    for gi, w in enumerate(POOL_WINDOWS):
        left = w // 2
        right = w - 1 - left
        cols = slice(gi * dg, (gi + 1) * dg)
        p = hext[:, cols]
        k = 1
        while k < w:
            p = p + pltpu.roll(p, n - k, 0)
            k *= 2
        wsum = pltpu.roll(p, left, 0)[8:8 + ts]
        cnt = (jnp.minimum(pos + right + 1, seq) - jnp.maximum(pos - left, 0)).astype(F32)
        pooled = (wsum / cnt - h[:, cols]).astype(BF16)
        y = jnp.dot(pooled, w_ref[gi], preferred_element_type=F32) * sc_ref[:, cols]
        o_ref[:, cols] = x[:, cols] + y


def _pool_layer(x, gain, pool_w, scale, layer, seq, ts):
    m, d = x.shape
    ng, dg = pool_w.shape[1], pool_w.shape[2]
    prev, nxt = _halo_specs(ts, m, d, 1)
    return pl.pallas_call(
        functools.partial(_pool_body, ts=ts, seq=seq),
        grid=(m // ts,),
        in_specs=[
            prev,
            pl.BlockSpec((ts, d), lambda i: (i, 0)),
            nxt,
            pl.BlockSpec((1, d), lambda i: (0, 0)),
            pl.BlockSpec((None, ng, dg, dg), lambda i: (layer, 0, 0, 0)),
            pl.BlockSpec((1, d), lambda i: (0, 0)),
        ],
        out_specs=pl.BlockSpec((ts, d), lambda i: (i, 0)),
        out_shape=jax.ShapeDtypeStruct((m, d), F32),
        compiler_params=_params("arbitrary"),
        name="pool_mixer",
    )(x, x, x, gain, pool_w, scale)


def _proj_res_body(a_ref, w_ref, x_ref, o_ref):
    o_ref[...] = x_ref[...] + jnp.dot(a_ref[...], w_ref[...], preferred_element_type=F32)


def _proj_residual(a, w, x, tm):
    m, d = x.shape
    k = a.shape[1]
    return pl.pallas_call(
        _proj_res_body,
        grid=(m // tm,),
        in_specs=[
            pl.BlockSpec((tm, k), lambda i: (i, 0)),
            pl.BlockSpec((k, d), lambda i: (0, 0)),
            pl.BlockSpec((tm, d), lambda i: (i, 0)),
        ],
        out_specs=pl.BlockSpec((tm, d), lambda i: (i, 0)),
        out_shape=jax.ShapeDtypeStruct((m, d), F32),
        compiler_params=_params("arbitrary"),
        name="proj_residual",
    )(a, w, x)


def _group_sumsq(z, e):
    s = z * z
    s_hi = s.astype(BF16)
    s_lo = (s - s_hi.astype(F32)).astype(BF16)
    return (jnp.dot(s_hi, e, preferred_element_type=F32) + jnp.dot(s_lo, e, preferred_element_type=F32))


def _swa_qkv_body(x_ref, g_ref, w_ref, qg_ref, kg_ref, e_ref, q_ref, k_ref, v_ref):
    hn = _rms(x_ref[...], g_ref[...]).astype(BF16)
    qkv = jnp.dot(hn, w_ref[...], preferred_element_type=F32)
    e = e_ref[...]
    nq = q_ref.shape[1]
    nkv = k_ref.shape[1]
    cw = e.shape[0]
    inv_hd = 1.0 / SWA_HEAD_DIM

    def head_norm(z, gain):
        return z * lax.rsqrt(_group_sumsq(z, e) * inv_hd + EPS) * gain

    qg = qg_ref[...] * (SWA_HEAD_DIM ** -0.5 * LOG2E)
    for c in range(nq // cw):
        cols = slice(c * cw, (c + 1) * cw)
        q_ref[:, cols] = head_norm(qkv[:, cols], qg).astype(BF16)
    k_ref[...] = head_norm(qkv[:, nq:nq + nkv], kg_ref[...]).astype(BF16)
    v_ref[...] = qkv[:, nq + nkv:].astype(BF16)


def _swa_qkv(x, gain, w_qkv, qg, kg, e, tm):
    m, d = x.shape
    nkv = SWA_KV_HEADS * SWA_HEAD_DIM
    nq = w_qkv.shape[1] - 2 * nkv
    return pl.pallas_call(
        _swa_qkv_body,
        grid=(m // tm,),
        in_specs=[
            pl.BlockSpec((tm, d), lambda i: (i, 0)),
            pl.BlockSpec((1, d), lambda i: (0, 0)),
            pl.BlockSpec(w_qkv.shape, lambda i: (0, 0)),
            pl.BlockSpec((1, nkv), lambda i: (0, 0)),
            pl.BlockSpec((1, nkv), lambda i: (0, 0)),
            pl.BlockSpec(e.shape, lambda i: (0, 0)),
        ],
        out_specs=[
            pl.BlockSpec((tm, nq), lambda i: (i, 0)),
            pl.BlockSpec((tm, nkv), lambda i: (i, 0)),
            pl.BlockSpec((tm, nkv), lambda i: (i, 0)),
        ],
        out_shape=[
            jax.ShapeDtypeStruct((m, nq), BF16),
            jax.ShapeDtypeStruct((m, nkv), BF16),
            jax.ShapeDtypeStruct((m, nkv), BF16),
        ],
        compiler_params=_params("arbitrary"),
        name="swa_qkv",
    )(x, gain, w_qkv, qg, kg, e)


def _swa_slopes():
    return [float(2.0 ** (-8.0 * (i + 1) / SWA_HEADS)) for i in range(SWA_HEADS)]


def _swa_attn_body(sink_ref, q_ref, k_ref, v_ref, pc_ref, pr_ref, o_ref, *, seq):
    j = pl.program_id(1)
    hd = SWA_HEAD_DIM
    start = pl.multiple_of(jnp.clip(j * SWA_BLOCK - SWA_WINDOW, 0, seq - SWA_SPAN), SWA_BLOCK)
    kblk = k_ref[0, pl.ds(start, SWA_SPAN), :]
    vblk = v_ref[0, pl.ds(start, SWA_SPAN), :]
    pk = pr_ref[:, pl.ds(start, SWA_SPAN)]
    pq = pc_ref[...]
    qi = j * SWA_BLOCK + lax.broadcasted_iota(jnp.int32, (SWA_BLOCK, 1), 0)
    ki = start + lax.broadcasted_iota(jnp.int32, (1, SWA_SPAN), 1)
    in_win = jnp.abs(qi - ki) <= SWA_WINDOW
    dist = jnp.where(in_win, jnp.abs(pq - pk).astype(F32), jnp.inf)
    lane = lax.broadcasted_iota(jnp.int32, (1, 2 * hd), 1)
    slopes = _swa_slopes()
    zero = jnp.zeros((), BF16)
    lane_full = lax.broadcasted_iota(jnp.int32, (SWA_SPAN, 2 * hd), 1)
    ind_a = jnp.where(lane_full < hd, 1.0, 0.0).astype(BF16)
    ind_b = jnp.where(lane_full >= hd, 1.0, 0.0).astype(BF16)
    for mp in range(SWA_KV_HEADS // 2):
        kp = kblk[:, mp * 2 * hd:(mp + 1) * 2 * hd]
        vp = vblk[:, mp * 2 * hd:(mp + 1) * 2 * hd]
        k2 = jnp.concatenate([jnp.where(lane < hd, kp, zero), jnp.where(lane >= hd, kp, zero)], axis=0)
        v2 = jnp.concatenate([jnp.concatenate([jnp.where(lane < hd, vp, zero), ind_a], axis=1),
                              jnp.concatenate([jnp.where(lane >= hd, vp, zero), ind_b], axis=1)], axis=0)
        pair_cols = [slice((mp * SWA_GROUP + g) * 2 * hd, (mp * SWA_GROUP + g + 1) * 2 * hd)
                     for g in range(SWA_GROUP)]
        q_all = jnp.concatenate([q_ref[0, :, cols] for cols in pair_cols], axis=0)
        s_all = lax.dot_general(q_all, k2, (((1,), (1,)), ((), ())), preferred_element_type=F32)
        ps, sink_terms = [], []
        for g in range(SWA_GROUP):
            ha = (2 * mp) * SWA_GROUP + g
            hb = (2 * mp + 1) * SWA_GROUP + g
            s = s_all[g * SWA_BLOCK:(g + 1) * SWA_BLOCK]
            sa = s[:, :SWA_SPAN] - (LOG2E * slopes[ha]) * dist
            sb = s[:, SWA_SPAN:] - (LOG2E * slopes[hb]) * dist
            sink_a = sink_ref[ha] * LOG2E
            sink_b = sink_ref[hb] * LOG2E
            ma = jnp.maximum(jnp.max(sa, axis=-1, keepdims=True), sink_a)
            mb = jnp.maximum(jnp.max(sb, axis=-1, keepdims=True), sink_b)
            ps.append(jnp.concatenate([jnp.exp2(sa - ma), jnp.exp2(sb - mb)], axis=1).astype(BF16))
            sink_terms.append(jnp.where(lane < hd, jnp.exp2(sink_a - ma), jnp.exp2(sink_b - mb)))
        o_all = jnp.dot(jnp.concatenate(ps, axis=0), v2, preferred_element_type=F32)
        for g in range(SWA_GROUP):
            og = o_all[g * SWA_BLOCK:(g + 1) * SWA_BLOCK]
            o_ref[0, :, pair_cols[g]] = (og[:, :2 * hd] / (og[:, 2 * hd:] + sink_terms[g])).astype(BF16)


def _swa_attention(q, k, v, pos_col, pos_row, sinks, seq):
    b = q.shape[0]
    nq = q.shape[2]
    nkv = k.shape[2]
    return pl.pallas_call(
        functools.partial(_swa_attn_body, seq=seq),
        grid=(b, seq // SWA_BLOCK),
        in_specs=[
            pl.BlockSpec(memory_space=pltpu.SMEM),
            pl.BlockSpec((1, SWA_BLOCK, nq), lambda bi, j: (bi, j, 0)),
            pl.BlockSpec((1, seq, nkv), lambda bi, j: (bi, 0, 0)),
            pl.BlockSpec((1, seq, nkv), lambda bi, j: (bi, 0, 0)),
            pl.BlockSpec((SWA_BLOCK, 1), lambda bi, j: (j, 0)),
            pl.BlockSpec((1, seq), lambda bi, j: (0, 0)),
        ],
        out_specs=pl.BlockSpec((1, SWA_BLOCK, nq), lambda bi, j: (bi, j, 0)),
        out_shape=jax.ShapeDtypeStruct(q.shape, BF16),
        compiler_params=_params("arbitrary", "arbitrary"),
        name="swa_attention",
    )(sinks, q, k, v, pos_col, pos_row)


def _rope64(z, c, s1, s2):
    return z * c + pltpu.roll(z, V7X_LANES - MLA_ROPE // 2, 1) * s1 + pltpu.roll(z, MLA_ROPE // 2, 1) * s2


def _slot_rms(z, gain):
    ms = jnp.sum(z * z, axis=-1, keepdims=True) * (1.0 / MLA_ROPE)
    return z * lax.rsqrt(ms + EPS) * gain


def _mla_down_body(x_ref, g_ref, w_ref, qa_ref, kva_ref, kr_ref, c_ref, s1_ref, s2_ref,
                   cq_ref, ckv_ref, kpe_ref):
    hn = _rms(x_ref[...], g_ref[...]).astype(BF16)
    d = jnp.dot(hn, w_ref[...], preferred_element_type=F32)
    cq_ref[...] = _rms(d[:, :MLA_Q_RANK], qa_ref[...]).astype(BF16)
    ckv_ref[...] = _rms(d[:, MLA_Q_RANK:MLA_Q_RANK + MLA_KV_RANK], kva_ref[...]).astype(BF16)
    kp = _slot_rms(d[:, MLA_Q_RANK + MLA_KV_RANK:], kr_ref[...])
    kpe_ref[...] = _rope64(kp, c_ref[...], s1_ref[...], s2_ref[...]).astype(BF16)


def _mla_down(x, gain, w_down, qa, kva, kr, c, s1, s2, seq, tm):
    m, d = x.shape
    nt = seq // tm
    tab = pl.BlockSpec((tm, V7X_LANES), lambda i: (i % nt, 0))
    row = lambda n: pl.BlockSpec((1, n), lambda i: (0, 0))
    return pl.pallas_call(
        _mla_down_body,
        grid=(m // tm,),
        in_specs=[
            pl.BlockSpec((tm, d), lambda i: (i, 0)),
            row(d),
            pl.BlockSpec(w_down.shape, lambda i: (0, 0)),
            row(MLA_Q_RANK), row(MLA_KV_RANK), row(V7X_LANES),
            tab, tab, tab,
        ],
        out_specs=[
            pl.BlockSpec((tm, MLA_Q_RANK), lambda i: (i, 0)),
            pl.BlockSpec((tm, MLA_KV_RANK), lambda i: (i, 0)),
            pl.BlockSpec((tm, V7X_LANES), lambda i: (i, 0)),
        ],
        out_shape=[
            jax.ShapeDtypeStruct((m, MLA_Q_RANK), BF16),
            jax.ShapeDtypeStruct((m, MLA_KV_RANK), BF16),
            jax.ShapeDtypeStruct((m, V7X_LANES), BF16),
        ],
        compiler_params=_params("arbitrary"),
        name="mla_down",
    )(x, gain, w_down, qa, kva, kr, c, s1, s2)


def _mla_up_body(cq_ref, ckv_ref, kpe_ref, wuqt_ref, wuk_ref, wuvt_ref, qn_ref, qr_ref, kn_ref,
                 cos_ref, sin_ref, qt_ref, k_ref, vt_ref):
    ckv = ckv_ref[...]
    nt = (((1,), (1,)), ((), ()))
    qt = lax.dot_general(wuqt_ref[...], cq_ref[...], nt, preferred_element_type=F32)
    vt = lax.dot_general(wuvt_ref[...], ckv, nt, preferred_element_type=F32)
    kn = jnp.dot(ckv, wuk_ref[...], preferred_element_type=F32)
    kpe = kpe_ref[...]
    qng = qn_ref[...]
    qrg = qr_ref[...]
    kng = kn_ref[...]
    cos, sin = cos_ref[...], sin_ref[...]
    tm = ckv.shape[0]
    half = MLA_ROPE // 2
    ones = jnp.ones((MLA_VT_ROWS - MLA_V, tm), BF16)
    zeros = jnp.zeros((MLA_SLOT - MLA_NOPE - MLA_ROPE, tm), BF16)
    for h in range(MLA_HEADS):
        lo = h * MLA_SLOT
        mid = lo + MLA_NOPE
        qn = qt[lo:mid]
        ms = jnp.mean(qn * qn, axis=0, keepdims=True)
        qt_ref[0, h, 0:MLA_NOPE, :] = (qn * lax.rsqrt(ms + EPS) * qng).astype(BF16)
        x1 = qt[mid:mid + half]
        x2 = qt[mid + half:mid + 2 * half]
        ms = (jnp.sum(x1 * x1, axis=0, keepdims=True)
              + jnp.sum(x2 * x2, axis=0, keepdims=True)) * (1.0 / MLA_ROPE)
        r = lax.rsqrt(ms + EPS)
        x1 = x1 * r * qrg[0:half]
        x2 = x2 * r * qrg[half:2 * half]
        qt_ref[0, h, MLA_NOPE:MLA_NOPE + half, :] = (x1 * cos - x2 * sin).astype(BF16)
        qt_ref[0, h, MLA_NOPE + half:MLA_NOPE + 2 * half, :] = (x2 * cos + x1 * sin).astype(BF16)
        qt_ref[0, h, MLA_NOPE + 2 * half:MLA_SLOT, :] = zeros
        k_ref[0, h, :, 0:MLA_NOPE] = _rms(kn[:, h * MLA_NOPE:(h + 1) * MLA_NOPE], kng).astype(BF16)
        k_ref[0, h, :, MLA_NOPE:MLA_SLOT] = kpe
        vt_ref[0, h, 0:MLA_V, :] = vt[h * MLA_V:(h + 1) * MLA_V, :].astype(BF16)
        vt_ref[0, h, MLA_V:MLA_VT_ROWS, :] = ones


def _mla_up(cq, ckv, kpe, w_uqt, w_uk, w_uvt, qn, qr, kn, cos_t, sin_t, b, seq, tm):
    nt = seq // tm
    tok = lambda n: pl.BlockSpec((tm, n), lambda bi, i: (bi * nt + i, 0))
    full = lambda a: pl.BlockSpec(a.shape, lambda bi, i: (0, 0))
    tab = pl.BlockSpec((MLA_ROPE // 2, tm), lambda bi, i: (0, i))
    out = pl.BlockSpec((1, MLA_HEADS, tm, MLA_SLOT), lambda bi, i: (bi, 0, i, 0))
    shape = jax.ShapeDtypeStruct((b, MLA_HEADS, seq, MLA_SLOT), BF16)
    out_q = pl.BlockSpec((1, MLA_HEADS, MLA_SLOT, tm), lambda bi, i: (bi, 0, 0, i))
    shape_q = jax.ShapeDtypeStruct((b, MLA_HEADS, MLA_SLOT, seq), BF16)
    out_v = pl.BlockSpec((1, MLA_HEADS, MLA_VT_ROWS, tm), lambda bi, i: (bi, 0, 0, i))
    shape_v = jax.ShapeDtypeStruct((b, MLA_HEADS, MLA_VT_ROWS, seq), BF16)
    return pl.pallas_call(
        _mla_up_body,
        grid=(b, nt),
        in_specs=[tok(MLA_Q_RANK), tok(MLA_KV_RANK), tok(V7X_LANES), full(w_uqt), full(w_uk), full(w_uvt),
                  full(qn), full(qr), full(kn), tab, tab],
        out_specs=[out_q, out, out_v],
        out_shape=[shape_q, shape, shape_v],
        compiler_params=_params("arbitrary", "arbitrary"),
        name="mla_up",
    )(cq, ckv, kpe, w_uqt, w_uk, w_uvt, qn, qr, kn, cos_t, sin_t)


def _mla_attn_body(q_ref, k_ref, vt_ref, o_ref, s0_ref, s1_ref, m0_ref, m1_ref):
    step = pl.program_id(0)

    @pl.when(step == 0)
    def _():
        s1_ref[...] = jnp.zeros(s1_ref.shape, F32)
        m1_ref[...] = jnp.zeros(m1_ref.shape, F32)

    def work(s_new, m_new, s_old, m_old):
        q = q_ref[0, 0]
        m_prev = m_old[...]
        nc = s_new.shape[0] // MLA_KEY_CHUNK
        chunk = lambda c: slice(c * MLA_KEY_CHUNK, (c + 1) * MLA_KEY_CHUNK)

        def probs(c):
            return jnp.exp2(s_old[chunk(c), :] - m_prev).astype(BF16)

        def weigh(c, p, ot):
            part = jnp.dot(vt_ref[0, 0, :, chunk(c)], p, preferred_element_type=F32)
            return part if ot is None else ot + part

        def score(c, m_run):
            st = jnp.dot(k_ref[0, 0, chunk(c), :], q, preferred_element_type=F32)
            s_new[chunk(c), :] = st
            mc = jnp.max(st, axis=0, keepdims=True)
            return mc if m_run is None else jnp.maximum(m_run, mc)

        ot = weigh(0, probs(0), None)
        m_run = None
        for c in range(nc - 1):
            m_run = score(c, m_run)
            ot = weigh(c + 1, probs(c + 1), ot)
        o = ot[0:MLA_V] / ot[MLA_V:MLA_V + 1]
        o_ref[0] = o.T.astype(BF16)
        m_new[...] = score(nc - 1, m_run)

    @pl.when(step % 2 == 0)
    def _():
        work(s0_ref, m0_ref, s1_ref, m1_ref)

    @pl.when(step % 2 == 1)
    def _():
        work(s1_ref, m1_ref, s0_ref, m0_ref)


def _mla_attention(qt, k, vt, tq):
    b, nh, seq, slot = k.shape
    nq = seq // tq
    n = b * nh * nq

    def tile(t):
        return t // (nh * nq), (t // nq) % nh, t % nq

    def q_map(s):
        bi, h, i = tile(jnp.minimum(s, n - 1))
        return bi, h, 0, i

    def k_map(s):
        bi, h, _ = tile(jnp.minimum(s, n - 1))
        return bi, h, 0, 0

    def vt_map(s):
        bi, h, _ = tile(jnp.maximum(s - 1, 0))
        return bi, h, 0, 0

    def o_map(s):
        bi, h, i = tile(jnp.maximum(s - 1, 0))
        return bi, i, h

    return pl.pallas_call(
        _mla_attn_body,
        grid=(n + 1,),
        in_specs=[
            pl.BlockSpec((1, 1, slot, tq), q_map),
            pl.BlockSpec((1, 1, seq, slot), k_map),
            pl.BlockSpec((1, 1, MLA_VT_ROWS, seq), vt_map),
        ],
        out_specs=pl.BlockSpec((1, tq, MLA_V), o_map),
        out_shape=jax.ShapeDtypeStruct((b, seq, nh * MLA_V), BF16),
        scratch_shapes=[pltpu.VMEM((seq, tq), F32), pltpu.VMEM((seq, tq), F32),
                        pltpu.VMEM((1, tq), F32), pltpu.VMEM((1, tq), F32)],
        compiler_params=_params("arbitrary"),
        name="mla_attention",
    )(qt, k, vt)


def _swa_pair_perm(w_q_cols):
    lead = w_q_cols.shape[:-1]
    w = w_q_cols.reshape(lead + (2, 2, SWA_GROUP, SWA_HEAD_DIM))
    w = jnp.swapaxes(w, -3, -2)
    return w.reshape(lead + (SWA_HEADS * SWA_HEAD_DIM,))


def _rope_tables(positions):
    half = MLA_ROPE // 2
    inv = ROPE_THETA ** (-jnp.arange(0, MLA_ROPE, 2, dtype=F32) / MLA_ROPE)
    ang = positions.astype(F32)[:, None] * inv[None, :]
    cos, sin = jnp.cos(ang), jnp.sin(ang)
    z = jnp.zeros_like(cos)
    pad = jnp.zeros((positions.shape[0], V7X_LANES - MLA_ROPE), F32)
    c = jnp.concatenate([cos, cos, pad], axis=1)
    s1 = jnp.concatenate([-sin, z, pad], axis=1)
    s2 = jnp.concatenate([z, sin, pad], axis=1)
    return c, s1, s2, cos.T, sin.T


def _pad_lanes(v, n):
    return jnp.pad(v, ((0, 0), (0, n - v.shape[1])))


def kernel(x, positions, norm_mix_g, norm_ffn_g, pool_w, pool_scale, swa_w_qkv, swa_q_gain, swa_k_gain, swa_sinks, swa_w_o, mla_w_down, mla_q_a_gain, mla_kv_a_gain, mla_w_uq, mla_w_ukv, mla_qn_gain, mla_qr_gain, mla_kn_gain, mla_kr_gain, mla_w_o, ffn_w_in, ffn_conv_w, ffn_conv_b, ffn_w_out):
    b, seq, d = x.shape
    m = b * seq
    depth = norm_mix_g.shape[0]
    t = _tiles(m, seq)
    xf = x.reshape(m, d)

    w_in = ffn_w_in.astype(BF16)
    w_out = ffn_w_out.astype(BF16)
    f = ffn_w_out.shape[1]
    cwb = jnp.concatenate([ffn_conv_w, ffn_conv_b[:, None, :], jnp.zeros((depth, 4, f), F32)], axis=1)
    pool_wb = pool_w.astype(BF16)

    nq = SWA_HEADS * SWA_HEAD_DIM
    pos_col = positions.reshape(seq, 1)
    pos_row = positions.reshape(1, seq)
    blk = np.kron(np.eye(4, dtype=np.float32), np.ones((SWA_HEAD_DIM, SWA_HEAD_DIM), np.float32))
    e_blk = jnp.asarray(blk, dtype=BF16)
    c_tab, s1_tab, s2_tab, cos_t, sin_t = _rope_tables(positions)

    for i in range(depth):
        kind = i % N_MIXERS
        j = i // N_MIXERS
        gain = norm_mix_g[i][None, :]
        if kind == 0:
            xf = _pool_layer(xf, gain, pool_wb, pool_scale[j][None, :], j, seq, t["pool_ts"])
        elif kind == 1:
            w = swa_w_qkv[j].astype(BF16)
            w_qkv = jnp.concatenate([_swa_pair_perm(w[:, :nq]), w[:, nq:]], axis=1)
            w_o = _swa_pair_perm(swa_w_o[j].astype(BF16).T).T
            qg = jnp.tile(swa_q_gain[j], 4)[None, :]
            kg = jnp.tile(swa_k_gain[j], 4)[None, :]
            q, k, v = _swa_qkv(xf, gain, w_qkv, qg, kg, e_blk, t["proj_tm"])
            o = _swa_attention(q.reshape(b, seq, -1), k.reshape(b, seq, -1), v.reshape(b, seq, -1),
                               pos_col, pos_row, swa_sinks[j], seq)
            xf = _proj_residual(o.reshape(m, -1), w_o, xf, t["res_tm"])
        else:
            n_lat = MLA_Q_RANK + MLA_KV_RANK
            w_down = _pad_lanes(mla_w_down[j].astype(BF16), n_lat + V7X_LANES)
            wq = mla_w_uq[j].astype(BF16).reshape(MLA_Q_RANK, MLA_HEADS, MLA_NOPE + MLA_ROPE)
            wq = jnp.pad(wq, ((0, 0), (0, 0), (0, MLA_SLOT - MLA_NOPE - MLA_ROPE)))
            w_uqt = wq.reshape(MLA_Q_RANK, MLA_HEADS * MLA_SLOT).T
            wkv = mla_w_ukv[j].astype(BF16).reshape(MLA_KV_RANK, MLA_HEADS, MLA_NOPE + MLA_V)
            w_uk = wkv[:, :, :MLA_NOPE].reshape(MLA_KV_RANK, MLA_HEADS * MLA_NOPE)
            w_uvt = wkv[:, :, MLA_NOPE:].reshape(MLA_KV_RANK, MLA_HEADS * MLA_V).T
            up_tm = t["up_tm"]
            q_scale = (MLA_NOPE + MLA_ROPE) ** -0.5 * LOG2E
            qn = jnp.broadcast_to((mla_qn_gain[j] * q_scale)[:, None], (MLA_NOPE, up_tm))
            qr = jnp.broadcast_to((mla_qr_gain[j] * q_scale)[:, None], (MLA_ROPE, up_tm))
            kr = _pad_lanes(mla_kr_gain[j][None, :], V7X_LANES)
            cq, ckv, kpe = _mla_down(xf, gain, w_down, mla_q_a_gain[j][None, :], mla_kv_a_gain[j][None, :], kr,
                                     c_tab, s1_tab, s2_tab, seq, t["proj_tm"])
            qt, k, vt = _mla_up(cq, ckv, kpe, w_uqt, w_uk, w_uvt, qn, qr, mla_kn_gain[j][None, :],
                                cos_t, sin_t, b, seq, up_tm)
            o = _mla_attention(qt, k, vt, t["attn_tq"])
            xf = _proj_residual(o.reshape(m, -1), mla_w_o[j].astype(BF16), xf, t["res_tm"])
        xf = _ffn_layer(xf, norm_ffn_g[i][None, :], w_in, cwb, w_out, i, seq, t["ffn_tm"], t["ffn_tf"])
    return xf.reshape(b, seq, d)
```

```python
import functools

import numpy as np
import jax
import jax.numpy as jnp
from jax import lax
from jax.experimental import pallas as pl
from jax.experimental.pallas import tpu as pltpu

F32 = jnp.float32
BF16 = jnp.bfloat16

EPS = 1e-6
LOG2E = 1.4426950408889634
N_MIXERS = 3
POOL_WINDOWS = (2, 4, 8, 16)
POOL_HALO = 16
POOL_SUB = 128
POOL_BAND_K = 256
SWA_HEADS = 32
SWA_KV_HEADS = 4
SWA_GROUP = SWA_HEADS // SWA_KV_HEADS
SWA_HEAD_DIM = 64
SWA_WINDOW = 128
SWA_BLOCK = 128
SWA_SPAN = SWA_BLOCK + 2 * SWA_WINDOW
MLA_HEADS = 16
MLA_NOPE = 128
MLA_ROPE = 64
MLA_V = 128
MLA_Q_RANK = 512
MLA_KV_RANK = 512
MLA_SLOT = 256
MLA_VT_ROWS = MLA_V + 16
MLA_KEY_CHUNK = 512
ROPE_THETA = 10000.0

V7X_LANES = 128
V7X_SUBLANES = 8
V7X_VMEM_LIMIT_BYTES = 62 * 1024 * 1024


def _tiles(m, seq):
    def fit(t):
        while seq % t:
            t //= 2
        return t
    return dict(ffn_tm=fit(1024), ffn_tf=512, pool_ts=fit(512), proj_tm=fit(512), up_tm=fit(256),
                attn_tq=fit(1024), res_tm=fit(512))


def _params(*sem):
    return pltpu.CompilerParams(dimension_semantics=sem, vmem_limit_bytes=V7X_VMEM_LIMIT_BYTES)


def _rms(x, g):
    ms = jnp.mean(x * x, axis=-1, keepdims=True)
    return x * lax.rsqrt(ms + EPS) * g


def _halo_specs(tm, m, d, nargs, rows=V7X_SUBLANES):
    r = tm // rows
    last = m // rows - 1
    if nargs == 1:
        prev = pl.BlockSpec((rows, d), lambda i: (jnp.maximum(i * r - 1, 0), 0))
        nxt = pl.BlockSpec((rows, d), lambda i: (jnp.minimum((i + 1) * r, last), 0))
    else:
        prev = pl.BlockSpec((rows, d), lambda i, j: (jnp.maximum(i * r - 1, 0), 0))
        nxt = pl.BlockSpec((rows, d), lambda i, j: (jnp.minimum((i + 1) * r, last), 0))
    return prev, nxt


def _ffn_body(xp_ref, x_ref, xn_ref, g_ref, wg_ref, wv_ref, cwb_ref, wo_ref, o_ref, hn_ref, *, tm, seq):
    i = pl.program_id(0)
    j = pl.program_id(1)

    @pl.when(j == 0)
    def _():
        g = g_ref[...]
        x = x_ref[...]
        hn_ref[0:tm, :] = _rms(x, g).astype(BF16)
        s0 = (i * tm) % seq
        keep_prev = jnp.where(s0 != 0, 7, -1)
        keep_next = jnp.where(s0 + tm != seq, 8, -1)
        ext = jnp.concatenate([_rms(xp_ref[...], g), _rms(xn_ref[...], g)], axis=0)
        r = lax.broadcasted_iota(jnp.int32, (16, 1), 0)
        ext = jnp.where((r == keep_prev) | (r == keep_next), ext, 0.0)
        hn_ref[tm:tm + 16, :] = ext.astype(BF16)
        o_ref[...] = x

    ug = jnp.dot(hn_ref[...], wg_ref[...], preferred_element_type=F32)
    uv = jnp.dot(hn_ref[0:tm, :], wv_ref[...], preferred_element_type=F32)
    gm = ug[0:tm]
    g_prev = ug[tm + 7:tm + 8]
    g_next = ug[tm + 8:tm + 9]
    rows = lax.broadcasted_iota(jnp.int32, (tm, 1), 0)
    up = jnp.where(rows == 0, g_prev, pltpu.roll(gm, 1, 0))
    dn = jnp.where(rows == tm - 1, g_next, pltpu.roll(gm, tm - 1, 0))
    cwb = cwb_ref[...]
    gc = up * cwb[0:1] + gm * cwb[1:2] + dn * cwb[2:3] + cwb[3:4]
    act = (gc * jax.nn.sigmoid(gc) * uv).astype(BF16)
    o_ref[...] += jnp.dot(act, wo_ref[...], preferred_element_type=F32)


def _ffn_layer(x, gain, w_in, cwb, w_out, layer, seq, tm, tf):
    m, d = x.shape
    f = w_out.shape[1]
    nf = f // tf
    prev, nxt = _halo_specs(tm, m, d, 2)
    return pl.pallas_call(
        functools.partial(_ffn_body, tm=tm, seq=seq),
        grid=(m // tm, nf),
        in_specs=[
            prev,
            pl.BlockSpec((tm, d), lambda i, j: (i, 0)),
            nxt,
            pl.BlockSpec((1, d), lambda i, j: (0, 0)),
            pl.BlockSpec((None, d, tf), lambda i, j: (layer, 0, j)),
            pl.BlockSpec((None, d, tf), lambda i, j: (layer, 0, j + nf)),
            pl.BlockSpec((None, 8, tf), lambda i, j: (layer, 0, j)),
            pl.BlockSpec((None, tf, d), lambda i, j: (layer, j, 0)),
        ],
        out_specs=pl.BlockSpec((tm, d), lambda i, j: (i, 0)),
        out_shape=jax.ShapeDtypeStruct((m, d), F32),
        scratch_shapes=[pltpu.VMEM((tm + 16, d), BF16)],
        compiler_params=_params("arbitrary", "arbitrary"),
        name="ffn_convglu",
    )(x, x, x, gain, w_in, w_in, cwb, w_out)


def _pool_bands():
    t = np.arange(POOL_SUB)[:, None]
    s = np.arange(POOL_BAND_K)[None, :]
    bands = [((s >= t + POOL_HALO - w // 2) & (s <= t + POOL_HALO + (w - 1 - w // 2))) for w in POOL_WINDOWS]
    return np.stack(bands).astype(np.float32)


def _pool_body(xp_ref, x_ref, xn_ref, g_ref, band_ref, w_ref, sc_ref, o_ref, hb_ref, *, ts, seq):
    i = pl.program_id(0)
    data_rows = ts + 2 * POOL_HALO

    @pl.when(i == 0)
    def _():
        hb_ref[data_rows:, :] = jnp.zeros((hb_ref.shape[0] - data_rows, hb_ref.shape[1]), BF16)

    g = g_ref[...]
    x = x_ref[...]
    s0 = (i * ts) % seq
    h = _rms(x, g)

    hb_ref[0:POOL_HALO, :] = (_rms(xp_ref[...], g) * jnp.where(s0 != 0, 1.0, 0.0)).astype(BF16)
    hb_ref[POOL_HALO:POOL_HALO + ts, :] = h.astype(BF16)
    hb_ref[POOL_HALO + ts:data_rows, :] = (_rms(xn_ref[...], g) * jnp.where(s0 + ts != seq, 1.0, 0.0)).astype(BF16)

    pos = s0 + lax.broadcasted_iota(jnp.int32, (ts, 1), 0)
    dg = x.shape[1] // len(POOL_WINDOWS)
    for gi, w in enumerate(POOL_WINDOWS):
        left = w // 2
        right = w - 1 - left
        cols = slice(gi * dg, (gi + 1) * dg)
        band = band_ref[gi]
        wsum = []
        for r0 in range(0, ts, POOL_SUB):
            rows = slice(r0, r0 + POOL_BAND_K)
            wsum.append(jnp.dot(band, hb_ref[rows, cols], preferred_element_type=F32))
        wsum = jnp.concatenate(wsum, axis=0)
        cnt = (jnp.minimum(pos + right + 1, seq) - jnp.maximum(pos - left, 0)).astype(F32)
        pooled = (wsum * (1.0 / cnt) - h[:, cols]).astype(BF16)
        y = jnp.dot(pooled, w_ref[gi], preferred_element_type=F32) * sc_ref[:, cols]
        o_ref[:, cols] = x[:, cols] + y


def _pool_layer(x, gain, bands, pool_w, scale, layer, seq, ts):
    m, d = x.shape
    ng, dg = pool_w.shape[1], pool_w.shape[2]
    prev, nxt = _halo_specs(ts, m, d, 1, POOL_HALO)
    scratch = pltpu.VMEM((ts + POOL_BAND_K - POOL_SUB, d), BF16)
    return pl.pallas_call(
        functools.partial(_pool_body, ts=ts, seq=seq),
        grid=(m // ts,),
        in_specs=[
            prev,
            pl.BlockSpec((ts, d), lambda i: (i, 0)),
            nxt,
            pl.BlockSpec((1, d), lambda i: (0, 0)),
            pl.BlockSpec(bands.shape, lambda i: (0, 0, 0)),
            pl.BlockSpec((None, ng, dg, dg), lambda i: (layer, 0, 0, 0)),
            pl.BlockSpec((1, d), lambda i: (0, 0)),
        ],
        out_specs=pl.BlockSpec((ts, d), lambda i: (i, 0)),
        out_shape=jax.ShapeDtypeStruct((m, d), F32),
        scratch_shapes=[scratch],
        compiler_params=_params("arbitrary"),
        name="pool_mixer",
    )(x, x, x, gain, bands, pool_w, scale)


def _proj_res_body(a_ref, w_ref, x_ref, o_ref):
    o_ref[...] = x_ref[...] + jnp.dot(a_ref[...], w_ref[...], preferred_element_type=F32)


def _proj_residual(a, w, x, tm):
    m, d = x.shape
    k = a.shape[1]
    return pl.pallas_call(
        _proj_res_body,
        grid=(m // tm,),
        in_specs=[
            pl.BlockSpec((tm, k), lambda i: (i, 0)),
            pl.BlockSpec((k, d), lambda i: (0, 0)),
            pl.BlockSpec((tm, d), lambda i: (i, 0)),
        ],
        out_specs=pl.BlockSpec((tm, d), lambda i: (i, 0)),
        out_shape=jax.ShapeDtypeStruct((m, d), F32),
        compiler_params=_params("arbitrary"),
        name="proj_residual",
    )(a, w, x)


def _group_sumsq(z, e):
    return jnp.dot((z * z).astype(BF16), e, preferred_element_type=F32)


def _swa_qkv_body(x_ref, g_ref, w_ref, qg_ref, kg_ref, e_ref, q_ref, k_ref, v_ref):
    hn = _rms(x_ref[...], g_ref[...]).astype(BF16)
    qkv = jnp.dot(hn, w_ref[...], preferred_element_type=F32)
    e = e_ref[...]
    nq = q_ref.shape[1]
    nkv = k_ref.shape[1]
    cw = e.shape[0]
    inv_hd = 1.0 / SWA_HEAD_DIM

    def head_norm(z, gain):
        return z * lax.rsqrt(_group_sumsq(z, e) * inv_hd + EPS) * gain

    qg = qg_ref[...] * (SWA_HEAD_DIM ** -0.5 * LOG2E)
    for c in range(nq // cw):
        cols = slice(c * cw, (c + 1) * cw)
        q_ref[:, cols] = head_norm(qkv[:, cols], qg).astype(BF16)
    k_ref[...] = head_norm(qkv[:, nq:nq + nkv], kg_ref[...]).astype(BF16)
    v_ref[...] = qkv[:, nq + nkv:].astype(BF16)


def _swa_qkv(x, gain, w_qkv, qg, kg, e, tm):
    m, d = x.shape
    nkv = SWA_KV_HEADS * SWA_HEAD_DIM
    nq = w_qkv.shape[1] - 2 * nkv
    return pl.pallas_call(
        _swa_qkv_body,
        grid=(m // tm,),
        in_specs=[
            pl.BlockSpec((tm, d), lambda i: (i, 0)),
            pl.BlockSpec((1, d), lambda i: (0, 0)),
            pl.BlockSpec(w_qkv.shape, lambda i: (0, 0)),
            pl.BlockSpec((1, nkv), lambda i: (0, 0)),
            pl.BlockSpec((1, nkv), lambda i: (0, 0)),
            pl.BlockSpec(e.shape, lambda i: (0, 0)),
        ],
        out_specs=[
            pl.BlockSpec((tm, nq), lambda i: (i, 0)),
            pl.BlockSpec((tm, nkv), lambda i: (i, 0)),
            pl.BlockSpec((tm, nkv), lambda i: (i, 0)),
        ],
        out_shape=[
            jax.ShapeDtypeStruct((m, nq), BF16),
            jax.ShapeDtypeStruct((m, nkv), BF16),
            jax.ShapeDtypeStruct((m, nkv), BF16),
        ],
        compiler_params=_params("arbitrary"),
        name="swa_qkv",
    )(x, gain, w_qkv, qg, kg, e)


def _swa_slopes():
    return [float(2.0 ** (-8.0 * (i + 1) / SWA_HEADS)) for i in range(SWA_HEADS)]


def _swa_attn_body(sink_ref, q_ref, k_ref, v_ref, pc_ref, pr_ref, o_ref, *, seq):
    j = pl.program_id(1)
    hd = SWA_HEAD_DIM
    start = pl.multiple_of(jnp.clip(j * SWA_BLOCK - SWA_WINDOW, 0, seq - SWA_SPAN), SWA_BLOCK)
    kblk = k_ref[0, pl.ds(start, SWA_SPAN), :]
    vblk = v_ref[0, pl.ds(start, SWA_SPAN), :]
    pk = pr_ref[:, pl.ds(start, SWA_SPAN)]
    pq = pc_ref[...]
    qi = j * SWA_BLOCK + lax.broadcasted_iota(jnp.int32, (SWA_BLOCK, 1), 0)
    ki = start + lax.broadcasted_iota(jnp.int32, (1, SWA_SPAN), 1)
    in_win = jnp.abs(qi - ki) <= SWA_WINDOW
    dist = jnp.where(in_win, jnp.abs(pq - pk).astype(F32), jnp.inf)
    lane = lax.broadcasted_iota(jnp.int32, (1, 2 * hd), 1)
    slopes = _swa_slopes()
    zero = jnp.zeros((), BF16)
    lane_full = lax.broadcasted_iota(jnp.int32, (SWA_SPAN, 2 * hd), 1)
    ind_a = jnp.where(lane_full < hd, 1.0, 0.0).astype(BF16)
    ind_b = jnp.where(lane_full >= hd, 1.0, 0.0).astype(BF16)
    def pair_cols(mp):
        return [slice((mp * SWA_GROUP + g) * 2 * hd, (mp * SWA_GROUP + g + 1) * 2 * hd) for g in range(SWA_GROUP)]

    def score(mp):
        kp = kblk[:, mp * 2 * hd:(mp + 1) * 2 * hd]
        k2 = jnp.concatenate([jnp.where(lane < hd, kp, zero), jnp.where(lane >= hd, kp, zero)], axis=0)
        q_all = jnp.concatenate([q_ref[0, :, cols] for cols in pair_cols(mp)], axis=0)
        return lax.dot_general(q_all, k2, (((1,), (1,)), ((), ())), preferred_element_type=F32)

    def softmax(mp, s_all):
        ps, sink_terms = [], []
        for g in range(SWA_GROUP):
            ha = (2 * mp) * SWA_GROUP + g
            hb = (2 * mp + 1) * SWA_GROUP + g
            s = s_all[g * SWA_BLOCK:(g + 1) * SWA_BLOCK]
            sa = s[:, :SWA_SPAN] - (LOG2E * slopes[ha]) * dist
            sb = s[:, SWA_SPAN:] - (LOG2E * slopes[hb]) * dist
            sink_a = sink_ref[ha] * LOG2E
            sink_b = sink_ref[hb] * LOG2E
            ma = jnp.maximum(jnp.max(sa, axis=-1, keepdims=True), sink_a)
            mb = jnp.maximum(jnp.max(sb, axis=-1, keepdims=True), sink_b)
            ps.append(jnp.concatenate([jnp.exp2(sa - ma), jnp.exp2(sb - mb)], axis=1).astype(BF16))
            sink_terms.append(jnp.where(lane < hd, jnp.exp2(sink_a - ma), jnp.exp2(sink_b - mb)))
        return jnp.concatenate(ps, axis=0), sink_terms

    def weigh(mp, p_all, sink_terms):
        vp = vblk[:, mp * 2 * hd:(mp + 1) * 2 * hd]
        v2 = jnp.concatenate([jnp.concatenate([jnp.where(lane < hd, vp, zero), ind_a], axis=1),
                              jnp.concatenate([jnp.where(lane >= hd, vp, zero), ind_b], axis=1)], axis=0)
        o_all = jnp.dot(p_all, v2, preferred_element_type=F32)
        for g, cols in enumerate(pair_cols(mp)):
            og = o_all[g * SWA_BLOCK:(g + 1) * SWA_BLOCK]
            o_ref[0, :, cols] = (og[:, :2 * hd] / (og[:, 2 * hd:] + sink_terms[g])).astype(BF16)

    for mp in range(SWA_KV_HEADS // 2):
        weigh(mp, *softmax(mp, score(mp)))


def _swa_attention(q, k, v, pos_col, pos_row, sinks, seq):
    b = q.shape[0]
    nq = q.shape[2]
    nkv = k.shape[2]
    return pl.pallas_call(
        functools.partial(_swa_attn_body, seq=seq),
        grid=(b, seq // SWA_BLOCK),
        in_specs=[
            pl.BlockSpec(memory_space=pltpu.SMEM),
            pl.BlockSpec((1, SWA_BLOCK, nq), lambda bi, j: (bi, j, 0)),
            pl.BlockSpec((1, seq, nkv), lambda bi, j: (bi, 0, 0)),
            pl.BlockSpec((1, seq, nkv), lambda bi, j: (bi, 0, 0)),
            pl.BlockSpec((SWA_BLOCK, 1), lambda bi, j: (j, 0)),
            pl.BlockSpec((1, seq), lambda bi, j: (0, 0)),
        ],
        out_specs=pl.BlockSpec((1, SWA_BLOCK, nq), lambda bi, j: (bi, j, 0)),
        out_shape=jax.ShapeDtypeStruct(q.shape, BF16),
        compiler_params=_params("arbitrary", "arbitrary"),
        name="swa_attention",
    )(sinks, q, k, v, pos_col, pos_row)


def _rope64(z, c, s1, s2):
    return z * c + pltpu.roll(z, V7X_LANES - MLA_ROPE // 2, 1) * s1 + pltpu.roll(z, MLA_ROPE // 2, 1) * s2


def _slot_rms(z, gain):
    ms = jnp.sum(z * z, axis=-1, keepdims=True) * (1.0 / MLA_ROPE)
    return z * lax.rsqrt(ms + EPS) * gain


def _mla_down_body(x_ref, g_ref, w_ref, qa_ref, kva_ref, kr_ref, c_ref, s1_ref, s2_ref,
                   cq_ref, ckv_ref, kpe_ref):
    hn = _rms(x_ref[...], g_ref[...]).astype(BF16)
    d = jnp.dot(hn, w_ref[...], preferred_element_type=F32)
    cq_ref[...] = _rms(d[:, :MLA_Q_RANK], qa_ref[...]).astype(BF16)
    ckv_ref[...] = _rms(d[:, MLA_Q_RANK:MLA_Q_RANK + MLA_KV_RANK], kva_ref[...]).astype(BF16)
    kp = _slot_rms(d[:, MLA_Q_RANK + MLA_KV_RANK:], kr_ref[...])
    kpe_ref[...] = _rope64(kp, c_ref[...], s1_ref[...], s2_ref[...]).astype(BF16)


def _mla_down(x, gain, w_down, qa, kva, kr, c, s1, s2, seq, tm):
    m, d = x.shape
    nt = seq // tm
    tab = pl.BlockSpec((tm, V7X_LANES), lambda i: (i % nt, 0))
    row = lambda n: pl.BlockSpec((1, n), lambda i: (0, 0))
    return pl.pallas_call(
        _mla_down_body,
        grid=(m // tm,),
        in_specs=[
            pl.BlockSpec((tm, d), lambda i: (i, 0)),
            row(d),
            pl.BlockSpec(w_down.shape, lambda i: (0, 0)),
            row(MLA_Q_RANK), row(MLA_KV_RANK), row(V7X_LANES),
            tab, tab, tab,
        ],
        out_specs=[
            pl.BlockSpec((tm, MLA_Q_RANK), lambda i: (i, 0)),
            pl.BlockSpec((tm, MLA_KV_RANK), lambda i: (i, 0)),
            pl.BlockSpec((tm, V7X_LANES), lambda i: (i, 0)),
        ],
        out_shape=[
            jax.ShapeDtypeStruct((m, MLA_Q_RANK), BF16),
            jax.ShapeDtypeStruct((m, MLA_KV_RANK), BF16),
            jax.ShapeDtypeStruct((m, V7X_LANES), BF16),
        ],
        compiler_params=_params("arbitrary"),
        name="mla_down",
    )(x, gain, w_down, qa, kva, kr, c, s1, s2)


def _mla_up_body(cq_ref, ckv_ref, kpe_ref, wuqt_ref, wuk_ref, wuvt_ref, qn_ref, qr_ref, kn_ref,
                 cos_ref, sin_ref, qt_ref, k_ref, vt_ref):
    ckv = ckv_ref[...]
    nt = (((1,), (1,)), ((), ()))
    qt = lax.dot_general(wuqt_ref[...], cq_ref[...], nt, preferred_element_type=F32)
    vt = lax.dot_general(wuvt_ref[...], ckv, nt, preferred_element_type=F32)
    kn = jnp.dot(ckv, wuk_ref[...], preferred_element_type=F32)
    kpe = kpe_ref[...]
    qng = qn_ref[...]
    qrg = qr_ref[...]
    kng = kn_ref[...]
    cos, sin = cos_ref[...], sin_ref[...]
    tm = ckv.shape[0]
    half = MLA_ROPE // 2
    ones = jnp.ones((MLA_VT_ROWS - MLA_V, tm), BF16)
    zeros = jnp.zeros((MLA_SLOT - MLA_NOPE - MLA_ROPE, tm), BF16)
    for h in range(MLA_HEADS):
        lo = h * MLA_SLOT
        mid = lo + MLA_NOPE
        qn = qt[lo:mid]
        ms = jnp.mean(qn * qn, axis=0, keepdims=True)
        qt_ref[0, h, 0:MLA_NOPE, :] = (qn * lax.rsqrt(ms + EPS) * qng).astype(BF16)
        x1 = qt[mid:mid + half]
        x2 = qt[mid + half:mid + 2 * half]
        ms = (jnp.sum(x1 * x1, axis=0, keepdims=True)
              + jnp.sum(x2 * x2, axis=0, keepdims=True)) * (1.0 / MLA_ROPE)
        r = lax.rsqrt(ms + EPS)
        x1 = x1 * r * qrg[0:half]
        x2 = x2 * r * qrg[half:2 * half]
        qt_ref[0, h, MLA_NOPE:MLA_NOPE + half, :] = (x1 * cos - x2 * sin).astype(BF16)
        qt_ref[0, h, MLA_NOPE + half:MLA_NOPE + 2 * half, :] = (x2 * cos + x1 * sin).astype(BF16)
        qt_ref[0, h, MLA_NOPE + 2 * half:MLA_SLOT, :] = zeros
        k_ref[0, h, :, 0:MLA_NOPE] = _rms(kn[:, h * MLA_NOPE:(h + 1) * MLA_NOPE], kng).astype(BF16)
        k_ref[0, h, :, MLA_NOPE:MLA_SLOT] = kpe
        vt_ref[0, h, 0:MLA_V, :] = vt[h * MLA_V:(h + 1) * MLA_V, :].astype(BF16)
        vt_ref[0, h, MLA_V:MLA_VT_ROWS, :] = ones


def _mla_up(cq, ckv, kpe, w_uqt, w_uk, w_uvt, qn, qr, kn, cos_t, sin_t, b, seq, tm):
    nt = seq // tm
    tok = lambda n: pl.BlockSpec((tm, n), lambda bi, i: (bi * nt + i, 0))
    full = lambda a: pl.BlockSpec(a.shape, lambda bi, i: (0, 0))
    tab = pl.BlockSpec((MLA_ROPE // 2, tm), lambda bi, i: (0, i))
    out = pl.BlockSpec((1, MLA_HEADS, tm, MLA_SLOT), lambda bi, i: (bi, 0, i, 0))
    shape = jax.ShapeDtypeStruct((b, MLA_HEADS, seq, MLA_SLOT), BF16)
    out_q = pl.BlockSpec((1, MLA_HEADS, MLA_SLOT, tm), lambda bi, i: (bi, 0, 0, i))
    shape_q = jax.ShapeDtypeStruct((b, MLA_HEADS, MLA_SLOT, seq), BF16)
    out_v = pl.BlockSpec((1, MLA_HEADS, MLA_VT_ROWS, tm), lambda bi, i: (bi, 0, 0, i))
    shape_v = jax.ShapeDtypeStruct((b, MLA_HEADS, MLA_VT_ROWS, seq), BF16)
    return pl.pallas_call(
        _mla_up_body,
        grid=(b, nt),
        in_specs=[tok(MLA_Q_RANK), tok(MLA_KV_RANK), tok(V7X_LANES), full(w_uqt), full(w_uk), full(w_uvt),
                  full(qn), full(qr), full(kn), tab, tab],
        out_specs=[out_q, out, out_v],
        out_shape=[shape_q, shape, shape_v],
        compiler_params=_params("arbitrary", "arbitrary"),
        name="mla_up",
    )(cq, ckv, kpe, w_uqt, w_uk, w_uvt, qn, qr, kn, cos_t, sin_t)


def _mla_attn_body(q_ref, k_ref, vt_ref, o_ref, s0_ref, s1_ref, m0_ref, m1_ref):
    step = pl.program_id(0)

    @pl.when(step == 0)
    def _():
        s1_ref[...] = jnp.zeros(s1_ref.shape, F32)
        m1_ref[...] = jnp.zeros(m1_ref.shape, F32)

    def work(s_new, m_new, s_old, m_old):
        q = q_ref[0, 0]
        m_prev = m_old[...]
        nc = s_new.shape[0] // MLA_KEY_CHUNK
        chunk = lambda c: slice(c * MLA_KEY_CHUNK, (c + 1) * MLA_KEY_CHUNK)

        def probs(c):
            return jnp.exp2(s_old[chunk(c), :] - m_prev).astype(BF16)

        def weigh(c, p, ot):
            part = jnp.dot(vt_ref[0, 0, :, chunk(c)], p, preferred_element_type=F32)
            return part if ot is None else ot + part

        def score(c, m_run):
            st = jnp.dot(k_ref[0, 0, chunk(c), :], q, preferred_element_type=F32)
            s_new[chunk(c), :] = st
            mc = jnp.max(st, axis=0, keepdims=True)
            return mc if m_run is None else jnp.maximum(m_run, mc)

        ot = weigh(0, probs(0), None)
        m_run = None
        for c in range(nc - 1):
            m_run = score(c, m_run)
            ot = weigh(c + 1, probs(c + 1), ot)
        o = ot[0:MLA_V] / ot[MLA_V:MLA_V + 1]
        o_ref[0] = o.T.astype(BF16)
        m_new[...] = score(nc - 1, m_run)

    @pl.when(step % 2 == 0)
    def _():
        work(s0_ref, m0_ref, s1_ref, m1_ref)

    @pl.when(step % 2 == 1)
    def _():
        work(s1_ref, m1_ref, s0_ref, m0_ref)


def _mla_attention(qt, k, vt, tq):
    b, nh, seq, slot = k.shape
    nq = seq // tq
    n = b * nh * nq

    def tile(t):
        return t // (nh * nq), (t // nq) % nh, t % nq

    def q_map(s):
        bi, h, i = tile(jnp.minimum(s, n - 1))
        return bi, h, 0, i

    def k_map(s):
        bi, h, _ = tile(jnp.minimum(s, n - 1))
        return bi, h, 0, 0

    def vt_map(s):
        bi, h, _ = tile(jnp.maximum(s - 1, 0))
        return bi, h, 0, 0

    def o_map(s):
        bi, h, i = tile(jnp.maximum(s - 1, 0))
        return bi, i, h

    return pl.pallas_call(
        _mla_attn_body,
        grid=(n + 1,),
        in_specs=[
            pl.BlockSpec((1, 1, slot, tq), q_map),
            pl.BlockSpec((1, 1, seq, slot), k_map),
            pl.BlockSpec((1, 1, MLA_VT_ROWS, seq), vt_map),
        ],
        out_specs=pl.BlockSpec((1, tq, MLA_V), o_map),
        out_shape=jax.ShapeDtypeStruct((b, seq, nh * MLA_V), BF16),
        scratch_shapes=[pltpu.VMEM((seq, tq), F32), pltpu.VMEM((seq, tq), F32),
                        pltpu.VMEM((1, tq), F32), pltpu.VMEM((1, tq), F32)],
        compiler_params=_params("arbitrary"),
        name="mla_attention",
    )(qt, k, vt)


def _swa_pair_perm(w_q_cols):
    lead = w_q_cols.shape[:-1]
    w = w_q_cols.reshape(lead + (2, 2, SWA_GROUP, SWA_HEAD_DIM))
    w = jnp.swapaxes(w, -3, -2)
    return w.reshape(lead + (SWA_HEADS * SWA_HEAD_DIM,))


def _rope_tables(positions):
    half = MLA_ROPE // 2
    inv = ROPE_THETA ** (-jnp.arange(0, MLA_ROPE, 2, dtype=F32) / MLA_ROPE)
    ang = positions.astype(F32)[:, None] * inv[None, :]
    cos, sin = jnp.cos(ang), jnp.sin(ang)
    z = jnp.zeros_like(cos)
    pad = jnp.zeros((positions.shape[0], V7X_LANES - MLA_ROPE), F32)
    c = jnp.concatenate([cos, cos, pad], axis=1)
    s1 = jnp.concatenate([-sin, z, pad], axis=1)
    s2 = jnp.concatenate([z, sin, pad], axis=1)
    return c, s1, s2, cos.T, sin.T


def _pad_lanes(v, n):
    return jnp.pad(v, ((0, 0), (0, n - v.shape[1])))


def kernel(x, positions, norm_mix_g, norm_ffn_g, pool_w, pool_scale, swa_w_qkv, swa_q_gain, swa_k_gain, swa_sinks, swa_w_o, mla_w_down, mla_q_a_gain, mla_kv_a_gain, mla_w_uq, mla_w_ukv, mla_qn_gain, mla_qr_gain, mla_kn_gain, mla_kr_gain, mla_w_o, ffn_w_in, ffn_conv_w, ffn_conv_b, ffn_w_out):
    b, seq, d = x.shape
    m = b * seq
    depth = norm_mix_g.shape[0]
    t = _tiles(m, seq)
    xf = x.reshape(m, d)

    w_in = ffn_w_in.astype(BF16)
    w_out = ffn_w_out.astype(BF16)
    f = ffn_w_out.shape[1]
    cwb = jnp.concatenate([ffn_conv_w, ffn_conv_b[:, None, :], jnp.zeros((depth, 4, f), F32)], axis=1)
    pool_wb = pool_w.astype(BF16)
    pool_bands = jnp.asarray(_pool_bands(), dtype=BF16)

    nq = SWA_HEADS * SWA_HEAD_DIM
    pos_col = positions.reshape(seq, 1)
    pos_row = positions.reshape(1, seq)
    blk = np.kron(np.eye(4, dtype=np.float32), np.ones((SWA_HEAD_DIM, SWA_HEAD_DIM), np.float32))
    e_blk = jnp.asarray(blk, dtype=BF16)
    c_tab, s1_tab, s2_tab, cos_t, sin_t = _rope_tables(positions)

    for i in range(depth):
        kind = i % N_MIXERS
        j = i // N_MIXERS
        gain = norm_mix_g[i][None, :]
        if kind == 0:
            xf = _pool_layer(xf, gain, pool_bands, pool_wb, pool_scale[j][None, :], j, seq, t["pool_ts"])
        elif kind == 1:
            w = swa_w_qkv[j].astype(BF16)
            w_qkv = jnp.concatenate([_swa_pair_perm(w[:, :nq]), w[:, nq:]], axis=1)
            w_o = _swa_pair_perm(swa_w_o[j].astype(BF16).T).T
            qg = jnp.tile(swa_q_gain[j], 4)[None, :]
            kg = jnp.tile(swa_k_gain[j], 4)[None, :]
            q, k, v = _swa_qkv(xf, gain, w_qkv, qg, kg, e_blk, t["proj_tm"])
            o = _swa_attention(q.reshape(b, seq, -1), k.reshape(b, seq, -1), v.reshape(b, seq, -1),
                               pos_col, pos_row, swa_sinks[j], seq)
            xf = _proj_residual(o.reshape(m, -1), w_o, xf, t["res_tm"])
        else:
            n_lat = MLA_Q_RANK + MLA_KV_RANK
            w_down = _pad_lanes(mla_w_down[j].astype(BF16), n_lat + V7X_LANES)
            wq = mla_w_uq[j].astype(BF16).reshape(MLA_Q_RANK, MLA_HEADS, MLA_NOPE + MLA_ROPE)
            wq = jnp.pad(wq, ((0, 0), (0, 0), (0, MLA_SLOT - MLA_NOPE - MLA_ROPE)))
            w_uqt = wq.reshape(MLA_Q_RANK, MLA_HEADS * MLA_SLOT).T
            wkv = mla_w_ukv[j].astype(BF16).reshape(MLA_KV_RANK, MLA_HEADS, MLA_NOPE + MLA_V)
            w_uk = wkv[:, :, :MLA_NOPE].reshape(MLA_KV_RANK, MLA_HEADS * MLA_NOPE)
            w_uvt = wkv[:, :, MLA_NOPE:].reshape(MLA_KV_RANK, MLA_HEADS * MLA_V).T
            up_tm = t["up_tm"]
            q_scale = (MLA_NOPE + MLA_ROPE) ** -0.5 * LOG2E
            qn = jnp.broadcast_to((mla_qn_gain[j] * q_scale)[:, None], (MLA_NOPE, up_tm))
            qr = jnp.broadcast_to((mla_qr_gain[j] * q_scale)[:, None], (MLA_ROPE, up_tm))
            kr = _pad_lanes(mla_kr_gain[j][None, :], V7X_LANES)
            cq, ckv, kpe = _mla_down(xf, gain, w_down, mla_q_a_gain[j][None, :], mla_kv_a_gain[j][None, :], kr,
                                     c_tab, s1_tab, s2_tab, seq, t["proj_tm"])
            qt, k, vt = _mla_up(cq, ckv, kpe, w_uqt, w_uk, w_uvt, qn, qr, mla_kn_gain[j][None, :],
                                cos_t, sin_t, b, seq, up_tm)
            o = _mla_attention(qt, k, vt, t["attn_tq"])
            xf = _proj_residual(o.reshape(m, -1), mla_w_o[j].astype(BF16), xf, t["res_tm"])
        xf = _ffn_layer(xf, norm_ffn_g[i][None, :], w_in, cwb, w_out, i, seq, t["ffn_tm"], t["ffn_tf"])
    return xf.reshape(b, seq, d)
```

```python
import functools

import numpy as np
import jax
import jax.numpy as jnp
from jax import lax
from jax.experimental import pallas as pl
from jax.experimental.pallas import tpu as pltpu

F32 = jnp.float32
BF16 = jnp.bfloat16

EPS = 1e-6
LOG2E = 1.4426950408889634
N_MIXERS = 3
POOL_WINDOWS = (2, 4, 8, 16)
POOL_HALO = 16
POOL_SUB = 128
POOL_BAND_K = 256
SWA_HEADS = 32
SWA_KV_HEADS = 4
SWA_GROUP = SWA_HEADS // SWA_KV_HEADS
SWA_HEAD_DIM = 64
SWA_WINDOW = 128
SWA_BLOCK = 128
SWA_SPAN = SWA_BLOCK + 2 * SWA_WINDOW
MLA_HEADS = 16
MLA_NOPE = 128
MLA_ROPE = 64
MLA_V = 128
MLA_Q_RANK = 512
MLA_KV_RANK = 512
MLA_SLOT = 256
MLA_VT_ROWS = MLA_V + 16
MLA_KEY_CHUNK = 512
ROPE_THETA = 10000.0

V7X_LANES = 128
V7X_SUBLANES = 8
V7X_VMEM_LIMIT_BYTES = 62 * 1024 * 1024


def _tiles(m, seq):
    def fit(t):
        while seq % t:
            t //= 2
        return t
    return dict(ffn_tm=fit(1024), ffn_tf=512, pool_ts=fit(512), proj_tm=fit(512), up_tm=fit(256),
                attn_tq=fit(512), res_tm=fit(512))


def _params(*sem):
    return pltpu.CompilerParams(dimension_semantics=sem, vmem_limit_bytes=V7X_VMEM_LIMIT_BYTES)


def _rms(x, g):
    ms = jnp.mean(x * x, axis=-1, keepdims=True)
    return x * lax.rsqrt(ms + EPS) * g


def _halo_specs(tm, m, d, nargs, rows=V7X_SUBLANES):
    r = tm // rows
    last = m // rows - 1
    if nargs == 1:
        prev = pl.BlockSpec((rows, d), lambda i: (jnp.maximum(i * r - 1, 0), 0))
        nxt = pl.BlockSpec((rows, d), lambda i: (jnp.minimum((i + 1) * r, last), 0))
    else:
        prev = pl.BlockSpec((rows, d), lambda i, j: (jnp.maximum(i * r - 1, 0), 0))
        nxt = pl.BlockSpec((rows, d), lambda i, j: (jnp.minimum((i + 1) * r, last), 0))
    return prev, nxt


def _ffn_norm_tile(xp_ref, x_ref, xn_ref, g_ref, hn_ref, first_row, *, tm, seq):
    g = g_ref[...]
    hn_ref[0:tm, :] = _rms(x_ref[...], g).astype(BF16)
    s0 = first_row % seq
    keep_prev = jnp.where(s0 != 0, 7, -1)
    keep_next = jnp.where(s0 + tm != seq, 8, -1)
    ext = jnp.concatenate([_rms(xp_ref[...], g), _rms(xn_ref[...], g)], axis=0)
    r = lax.broadcasted_iota(jnp.int32, (16, 1), 0)
    ext = jnp.where((r == keep_prev) | (r == keep_next), ext, 0.0)
    hn_ref[tm:tm + 16, :] = ext.astype(BF16)


def _ffn_body(xp_ref, x_ref, xn_ref, g_ref, wg_ref, wv_ref, cwb_ref, wo_ref, o_ref, hn_ref, *, tm, seq):
    i = pl.program_id(0)
    j = pl.program_id(1)

    @pl.when(j == 0)
    def _():
        _ffn_norm_tile(xp_ref, x_ref, xn_ref, g_ref, hn_ref, i * tm, tm=tm, seq=seq)

    ug = jnp.dot(hn_ref[...], wg_ref[...], preferred_element_type=F32)
    uv = jnp.dot(hn_ref[0:tm, :], wv_ref[...].astype(BF16), preferred_element_type=F32)
    gm = ug[0:tm]
    g_prev = ug[tm + 7:tm + 8]
    g_next = ug[tm + 8:tm + 9]
    rows = lax.broadcasted_iota(jnp.int32, (tm, 1), 0)
    up = jnp.where(rows == 0, g_prev, pltpu.roll(gm, 1, 0))
    dn = jnp.where(rows == tm - 1, g_next, pltpu.roll(gm, tm - 1, 0))
    cwb = cwb_ref[...]
    gc = up * cwb[0:1] + gm * cwb[1:2] + dn * cwb[2:3] + cwb[3:4]
    act = (gc * jax.nn.sigmoid(gc) * uv).astype(BF16)
    acc = jnp.where(j == 0, x_ref[...], o_ref[...])
    o_ref[...] = acc + jnp.dot(act, wo_ref[...].astype(BF16), preferred_element_type=F32)


def _ffn_layer(x, gain, w_gate, w_in, cwb, w_out, layer, seq, tm, tf):
    m, d = x.shape
    f = w_out.shape[1]
    nf = f // tf
    prev, nxt = _halo_specs(tm, m, d, 2)
    return pl.pallas_call(
        functools.partial(_ffn_body, tm=tm, seq=seq),
        grid=(m // tm, nf),
        in_specs=[
            prev,
            pl.BlockSpec((tm, d), lambda i, j: (i, 0)),
            nxt,
            pl.BlockSpec((1, d), lambda i, j: (0, 0)),
            pl.BlockSpec((None, d, tf), lambda i, j: (layer, 0, j)),
            pl.BlockSpec((None, d, tf), lambda i, j: (layer, 0, j + nf)),
            pl.BlockSpec((None, 8, tf), lambda i, j: (layer, 0, j)),
            pl.BlockSpec((None, tf, d), lambda i, j: (layer, j, 0)),
        ],
        out_specs=pl.BlockSpec((tm, d), lambda i, j: (i, 0)),
        out_shape=jax.ShapeDtypeStruct((m, d), F32),
        scratch_shapes=[pltpu.VMEM((tm + 16, d), BF16)],
        compiler_params=_params("arbitrary", "arbitrary"),
        name="ffn_convglu",
    )(x, x, x, gain, w_gate, w_in, cwb, w_out)


def _pool_bands():
    t = np.arange(POOL_SUB)[:, None]
    s = np.arange(POOL_BAND_K)[None, :]
    bands = [((s >= t + POOL_HALO - w // 2) & (s <= t + POOL_HALO + (w - 1 - w // 2))) for w in POOL_WINDOWS]
    return np.stack(bands).astype(np.float32)


def _pool_body(xp_ref, x_ref, xn_ref, g_ref, band_ref, w_ref, sc_ref, o_ref, hb_ref, *, ts, seq):
    i = pl.program_id(0)
    data_rows = ts + 2 * POOL_HALO

    @pl.when(i == 0)
    def _():
        hb_ref[data_rows:, :] = jnp.zeros((hb_ref.shape[0] - data_rows, hb_ref.shape[1]), BF16)

    g = g_ref[...]
    x = x_ref[...]
    s0 = (i * ts) % seq
    h = _rms(x, g)

    hb_ref[0:POOL_HALO, :] = (_rms(xp_ref[...], g) * jnp.where(s0 != 0, 1.0, 0.0)).astype(BF16)
    hb_ref[POOL_HALO:POOL_HALO + ts, :] = h.astype(BF16)
    hb_ref[POOL_HALO + ts:data_rows, :] = (_rms(xn_ref[...], g) * jnp.where(s0 + ts != seq, 1.0, 0.0)).astype(BF16)

    pos = s0 + lax.broadcasted_iota(jnp.int32, (ts, 1), 0)
    dg = x.shape[1] // len(POOL_WINDOWS)
    for gi, w in enumerate(POOL_WINDOWS):
        left = w // 2
        right = w - 1 - left
        cols = slice(gi * dg, (gi + 1) * dg)
        band = band_ref[gi]
        wsum = []
        for r0 in range(0, ts, POOL_SUB):
            rows = slice(r0, r0 + POOL_BAND_K)
            wsum.append(jnp.dot(band, hb_ref[rows, cols], preferred_element_type=F32))
        wsum = jnp.concatenate(wsum, axis=0)
        cnt = (jnp.minimum(pos + right + 1, seq) - jnp.maximum(pos - left, 0)).astype(F32)
        pooled = (wsum * (1.0 / cnt) - h[:, cols]).astype(BF16)
        y = jnp.dot(pooled, w_ref[gi], preferred_element_type=F32) * sc_ref[:, cols]
        o_ref[:, cols] = x[:, cols] + y


def _pool_layer(x, gain, bands, pool_w, scale, layer, seq, ts):
    m, d = x.shape
    ng, dg = pool_w.shape[1], pool_w.shape[2]
    prev, nxt = _halo_specs(ts, m, d, 1, POOL_HALO)
    scratch = pltpu.VMEM((ts + POOL_BAND_K - POOL_SUB, d), BF16)
    return pl.pallas_call(
        functools.partial(_pool_body, ts=ts, seq=seq),
        grid=(m // ts,),
        in_specs=[
            prev,
            pl.BlockSpec((ts, d), lambda i: (i, 0)),
            nxt,
            pl.BlockSpec((1, d), lambda i: (0, 0)),
            pl.BlockSpec(bands.shape, lambda i: (0, 0, 0)),
            pl.BlockSpec((None, ng, dg, dg), lambda i: (layer, 0, 0, 0)),
            pl.BlockSpec((1, d), lambda i: (0, 0)),
        ],
        out_specs=pl.BlockSpec((ts, d), lambda i: (i, 0)),
        out_shape=jax.ShapeDtypeStruct((m, d), F32),
        scratch_shapes=[scratch],
        compiler_params=_params("arbitrary"),
        name="pool_mixer",
    )(x, x, x, gain, bands, pool_w, scale)


def _proj_res_body(a_ref, w_ref, x_ref, o_ref):
    o_ref[...] = x_ref[...] + jnp.dot(a_ref[...], w_ref[...], preferred_element_type=F32)


def _proj_residual(a, w, x, tm):
    m, d = x.shape
    k = a.shape[1]
    return pl.pallas_call(
        _proj_res_body,
        grid=(m // tm,),
        in_specs=[
            pl.BlockSpec((tm, k), lambda i: (i, 0)),
            pl.BlockSpec((k, d), lambda i: (0, 0)),
            pl.BlockSpec((tm, d), lambda i: (i, 0)),
        ],
        out_specs=pl.BlockSpec((tm, d), lambda i: (i, 0)),
        out_shape=jax.ShapeDtypeStruct((m, d), F32),
        compiler_params=_params("arbitrary"),
        name="proj_residual",
    )(a, w, x)


def _group_sumsq(z, e):
    return jnp.dot((z * z).astype(BF16), e, preferred_element_type=F32)


def _swa_qkv_body(x_ref, g_ref, w_ref, qg_ref, kg_ref, e_ref, q_ref, k_ref, v_ref):
    hn = _rms(x_ref[...], g_ref[...]).astype(BF16)
    qkv = jnp.dot(hn, w_ref[...], preferred_element_type=F32)
    e = e_ref[...]
    nq = q_ref.shape[1]
    nkv = k_ref.shape[1]
    cw = e.shape[0]
    inv_hd = 1.0 / SWA_HEAD_DIM

    def head_norm(z, gain):
        return z * lax.rsqrt(_group_sumsq(z, e) * inv_hd + EPS) * gain

    qg = qg_ref[...] * (SWA_HEAD_DIM ** -0.5 * LOG2E)
    for c in range(nq // cw):
        cols = slice(c * cw, (c + 1) * cw)
        q_ref[:, cols] = head_norm(qkv[:, cols], qg).astype(BF16)
    k_ref[...] = head_norm(qkv[:, nq:nq + nkv], kg_ref[...]).astype(BF16)
    v_ref[...] = qkv[:, nq + nkv:].astype(BF16)


def _swa_qkv(x, gain, w_qkv, qg, kg, e, tm):
    m, d = x.shape
    nkv = SWA_KV_HEADS * SWA_HEAD_DIM
    nq = w_qkv.shape[1] - 2 * nkv
    return pl.pallas_call(
        _swa_qkv_body,
        grid=(m // tm,),
        in_specs=[
            pl.BlockSpec((tm, d), lambda i: (i, 0)),
            pl.BlockSpec((1, d), lambda i: (0, 0)),
            pl.BlockSpec(w_qkv.shape, lambda i: (0, 0)),
            pl.BlockSpec((1, nkv), lambda i: (0, 0)),
            pl.BlockSpec((1, nkv), lambda i: (0, 0)),
            pl.BlockSpec(e.shape, lambda i: (0, 0)),
        ],
        out_specs=[
            pl.BlockSpec((tm, nq), lambda i: (i, 0)),
            pl.BlockSpec((tm, nkv), lambda i: (i, 0)),
            pl.BlockSpec((tm, nkv), lambda i: (i, 0)),
        ],
        out_shape=[
            jax.ShapeDtypeStruct((m, nq), BF16),
            jax.ShapeDtypeStruct((m, nkv), BF16),
            jax.ShapeDtypeStruct((m, nkv), BF16),
        ],
        compiler_params=_params("arbitrary"),
        name="swa_qkv",
    )(x, gain, w_qkv, qg, kg, e)


def _swa_slopes():
    return [float(2.0 ** (-8.0 * (i + 1) / SWA_HEADS)) for i in range(SWA_HEADS)]


def _swa_attn_body(sink_ref, q_ref, k_ref, v_ref, pc_ref, pr_ref, o_ref, *, seq):
    j = pl.program_id(1)
    hd = SWA_HEAD_DIM
    start = pl.multiple_of(jnp.clip(j * SWA_BLOCK - SWA_WINDOW, 0, seq - SWA_SPAN), SWA_BLOCK)
    kblk = k_ref[0, pl.ds(start, SWA_SPAN), :]
    vblk = v_ref[0, pl.ds(start, SWA_SPAN), :]
    pk = pr_ref[:, pl.ds(start, SWA_SPAN)]
    pq = pc_ref[...]
    qi = j * SWA_BLOCK + lax.broadcasted_iota(jnp.int32, (SWA_BLOCK, 1), 0)
    ki = start + lax.broadcasted_iota(jnp.int32, (1, SWA_SPAN), 1)
    in_win = jnp.abs(qi - ki) <= SWA_WINDOW
    dist = jnp.where(in_win, jnp.abs(pq - pk).astype(F32), jnp.inf)
    lane = lax.broadcasted_iota(jnp.int32, (1, 2 * hd), 1)
    slopes = _swa_slopes()
    zero = jnp.zeros((), BF16)
    lane_full = lax.broadcasted_iota(jnp.int32, (SWA_SPAN, 2 * hd), 1)
    ind_a = jnp.where(lane_full < hd, 1.0, 0.0).astype(BF16)
    ind_b = jnp.where(lane_full >= hd, 1.0, 0.0).astype(BF16)
    def pair_cols(mp):
        return [slice((mp * SWA_GROUP + g) * 2 * hd, (mp * SWA_GROUP + g + 1) * 2 * hd) for g in range(SWA_GROUP)]

    def score(mp):
        kp = kblk[:, mp * 2 * hd:(mp + 1) * 2 * hd]
        k2 = jnp.concatenate([jnp.where(lane < hd, kp, zero), jnp.where(lane >= hd, kp, zero)], axis=0)
        q_all = jnp.concatenate([q_ref[0, :, cols] for cols in pair_cols(mp)], axis=0)
        return lax.dot_general(q_all, k2, (((1,), (1,)), ((), ())), preferred_element_type=F32)

    def softmax(mp, s_all):
        ps, sink_terms = [], []
        for g in range(SWA_GROUP):
            ha = (2 * mp) * SWA_GROUP + g
            hb = (2 * mp + 1) * SWA_GROUP + g
            s = s_all[g * SWA_BLOCK:(g + 1) * SWA_BLOCK]
            sa = s[:, :SWA_SPAN] - (LOG2E * slopes[ha]) * dist
            sb = s[:, SWA_SPAN:] - (LOG2E * slopes[hb]) * dist
            sink_a = sink_ref[ha] * LOG2E
            sink_b = sink_ref[hb] * LOG2E
            ma = jnp.maximum(jnp.max(sa, axis=-1, keepdims=True), sink_a)
            mb = jnp.maximum(jnp.max(sb, axis=-1, keepdims=True), sink_b)
            ps.append(jnp.concatenate([jnp.exp2(sa - ma), jnp.exp2(sb - mb)], axis=1).astype(BF16))
            sink_terms.append(jnp.where(lane < hd, jnp.exp2(sink_a - ma), jnp.exp2(sink_b - mb)))
        return jnp.concatenate(ps, axis=0), sink_terms

    def weigh(mp, p_all, sink_terms):
        vp = vblk[:, mp * 2 * hd:(mp + 1) * 2 * hd]
        v2 = jnp.concatenate([jnp.concatenate([jnp.where(lane < hd, vp, zero), ind_a], axis=1),
                              jnp.concatenate([jnp.where(lane >= hd, vp, zero), ind_b], axis=1)], axis=0)
        o_all = jnp.dot(p_all, v2, preferred_element_type=F32)
        for g, cols in enumerate(pair_cols(mp)):
            og = o_all[g * SWA_BLOCK:(g + 1) * SWA_BLOCK]
            o_ref[0, :, cols] = (og[:, :2 * hd] / (og[:, 2 * hd:] + sink_terms[g])).astype(BF16)

    for mp in range(SWA_KV_HEADS // 2):
        weigh(mp, *softmax(mp, score(mp)))


def _swa_attention(q, k, v, pos_col, pos_row, sinks, seq):
    b = q.shape[0]
    nq = q.shape[2]
    nkv = k.shape[2]
    return pl.pallas_call(
        functools.partial(_swa_attn_body, seq=seq),
        grid=(b, seq // SWA_BLOCK),
        in_specs=[
            pl.BlockSpec(memory_space=pltpu.SMEM),
            pl.BlockSpec((1, SWA_BLOCK, nq), lambda bi, j: (bi, j, 0)),
            pl.BlockSpec((1, seq, nkv), lambda bi, j: (bi, 0, 0)),
            pl.BlockSpec((1, seq, nkv), lambda bi, j: (bi, 0, 0)),
            pl.BlockSpec((SWA_BLOCK, 1), lambda bi, j: (j, 0)),
            pl.BlockSpec((1, seq), lambda bi, j: (0, 0)),
        ],
        out_specs=pl.BlockSpec((1, SWA_BLOCK, nq), lambda bi, j: (bi, j, 0)),
        out_shape=jax.ShapeDtypeStruct(q.shape, BF16),
        compiler_params=_params("arbitrary", "arbitrary"),
        name="swa_attention",
    )(sinks, q, k, v, pos_col, pos_row)


def _rope64(z, c, s1, s2):
    return z * c + pltpu.roll(z, V7X_LANES - MLA_ROPE // 2, 1) * s1 + pltpu.roll(z, MLA_ROPE // 2, 1) * s2


def _slot_rms(z, gain):
    ms = jnp.sum(z * z, axis=-1, keepdims=True) * (1.0 / MLA_ROPE)
    return z * lax.rsqrt(ms + EPS) * gain


def _mla_down_body(x_ref, g_ref, w_ref, qa_ref, kva_ref, kr_ref, c_ref, s1_ref, s2_ref,
                   cq_ref, ckv_ref, kpe_ref):
    hn = _rms(x_ref[...], g_ref[...]).astype(BF16)
    d = jnp.dot(hn, w_ref[...], preferred_element_type=F32)
    cq_ref[...] = _rms(d[:, :MLA_Q_RANK], qa_ref[...]).astype(BF16)
    ckv_ref[...] = _rms(d[:, MLA_Q_RANK:MLA_Q_RANK + MLA_KV_RANK], kva_ref[...]).astype(BF16)
    kp = _slot_rms(d[:, MLA_Q_RANK + MLA_KV_RANK:], kr_ref[...])
    kpe_ref[...] = _rope64(kp, c_ref[...], s1_ref[...], s2_ref[...]).astype(BF16)


def _mla_down(x, gain, w_down, qa, kva, kr, c, s1, s2, seq, tm):
    m, d = x.shape
    nt = seq // tm
    tab = pl.BlockSpec((tm, V7X_LANES), lambda i: (i % nt, 0))
    row = lambda n: pl.BlockSpec((1, n), lambda i: (0, 0))
    return pl.pallas_call(
        _mla_down_body,
        grid=(m // tm,),
        in_specs=[
            pl.BlockSpec((tm, d), lambda i: (i, 0)),
            row(d),
            pl.BlockSpec(w_down.shape, lambda i: (0, 0)),
            row(MLA_Q_RANK), row(MLA_KV_RANK), row(V7X_LANES),
            tab, tab, tab,
        ],
        out_specs=[
            pl.BlockSpec((tm, MLA_Q_RANK), lambda i: (i, 0)),
            pl.BlockSpec((tm, MLA_KV_RANK), lambda i: (i, 0)),
            pl.BlockSpec((tm, V7X_LANES), lambda i: (i, 0)),
        ],
        out_shape=[
            jax.ShapeDtypeStruct((m, MLA_Q_RANK), BF16),
            jax.ShapeDtypeStruct((m, MLA_KV_RANK), BF16),
            jax.ShapeDtypeStruct((m, V7X_LANES), BF16),
        ],
        compiler_params=_params("arbitrary"),
        name="mla_down",
    )(x, gain, w_down, qa, kva, kr, c, s1, s2)


def _mla_up_body(cq_ref, ckv_ref, kpe_ref, wuqt_ref, wuk_ref, wuvt_ref, qn_ref, qr_ref, kn_ref,
                 cos_ref, sin_ref, qt_ref, k_ref, vt_ref):
    ckv = ckv_ref[...]
    nt = (((1,), (1,)), ((), ()))
    qt = lax.dot_general(wuqt_ref[...], cq_ref[...], nt, preferred_element_type=F32)
    vt = lax.dot_general(wuvt_ref[...], ckv, nt, preferred_element_type=F32)
    kn = jnp.dot(ckv, wuk_ref[...], preferred_element_type=F32)
    kpe = kpe_ref[...]
    qng = qn_ref[...]
    qrg = qr_ref[...]
    kng = kn_ref[...]
    cos, sin = cos_ref[...], sin_ref[...]
    tm = ckv.shape[0]
    half = MLA_ROPE // 2
    ones = jnp.ones((MLA_VT_ROWS - MLA_V, tm), BF16)
    zeros = jnp.zeros((MLA_SLOT - MLA_NOPE - MLA_ROPE, tm), BF16)
    for h in range(MLA_HEADS):
        lo = h * MLA_SLOT
        mid = lo + MLA_NOPE
        qn = qt[lo:mid]
        ms = jnp.mean(qn * qn, axis=0, keepdims=True)
        qt_ref[0, h, 0:MLA_NOPE, :] = (qn * lax.rsqrt(ms + EPS) * qng).astype(BF16)
        x1 = qt[mid:mid + half]
        x2 = qt[mid + half:mid + 2 * half]
        ms = (jnp.sum(x1 * x1, axis=0, keepdims=True)
              + jnp.sum(x2 * x2, axis=0, keepdims=True)) * (1.0 / MLA_ROPE)
        r = lax.rsqrt(ms + EPS)
        x1 = x1 * r * qrg[0:half]
        x2 = x2 * r * qrg[half:2 * half]
        qt_ref[0, h, MLA_NOPE:MLA_NOPE + half, :] = (x1 * cos - x2 * sin).astype(BF16)
        qt_ref[0, h, MLA_NOPE + half:MLA_NOPE + 2 * half, :] = (x2 * cos + x1 * sin).astype(BF16)
        qt_ref[0, h, MLA_NOPE + 2 * half:MLA_SLOT, :] = zeros
        k_ref[0, h, :, 0:MLA_NOPE] = _rms(kn[:, h * MLA_NOPE:(h + 1) * MLA_NOPE], kng).astype(BF16)
        k_ref[0, h, :, MLA_NOPE:MLA_SLOT] = kpe
        vt_ref[0, h, 0:MLA_V, :] = vt[h * MLA_V:(h + 1) * MLA_V, :].astype(BF16)
        vt_ref[0, h, MLA_V:MLA_VT_ROWS, :] = ones


def _mla_up(cq, ckv, kpe, w_uqt, w_uk, w_uvt, qn, qr, kn, cos_t, sin_t, b, seq, tm):
    nt = seq // tm
    tok = lambda n: pl.BlockSpec((tm, n), lambda bi, i: (bi * nt + i, 0))
    full = lambda a: pl.BlockSpec(a.shape, lambda bi, i: (0, 0))
    tab = pl.BlockSpec((MLA_ROPE // 2, tm), lambda bi, i: (0, i))
    out = pl.BlockSpec((1, MLA_HEADS, tm, MLA_SLOT), lambda bi, i: (bi, 0, i, 0))
    shape = jax.ShapeDtypeStruct((b, MLA_HEADS, seq, MLA_SLOT), BF16)
    out_q = pl.BlockSpec((1, MLA_HEADS, MLA_SLOT, tm), lambda bi, i: (bi, 0, 0, i))
    shape_q = jax.ShapeDtypeStruct((b, MLA_HEADS, MLA_SLOT, seq), BF16)
    out_v = pl.BlockSpec((1, MLA_HEADS, MLA_VT_ROWS, tm), lambda bi, i: (bi, 0, 0, i))
    shape_v = jax.ShapeDtypeStruct((b, MLA_HEADS, MLA_VT_ROWS, seq), BF16)
    return pl.pallas_call(
        _mla_up_body,
        grid=(b, nt),
        in_specs=[tok(MLA_Q_RANK), tok(MLA_KV_RANK), tok(V7X_LANES), full(w_uqt), full(w_uk), full(w_uvt),
                  full(qn), full(qr), full(kn), tab, tab],
        out_specs=[out_q, out, out_v],
        out_shape=[shape_q, shape, shape_v],
        compiler_params=_params("arbitrary", "arbitrary"),
        name="mla_up",
    )(cq, ckv, kpe, w_uqt, w_uk, w_uvt, qn, qr, kn, cos_t, sin_t)


def _mla_attn_body(q_ref, k_ref, vt_ref, o_ref, s0_ref, s1_ref, m0_ref, m1_ref):
    step = pl.program_id(0)

    @pl.when(step == 0)
    def _():
        s1_ref[...] = jnp.zeros(s1_ref.shape, F32)
        m1_ref[...] = jnp.zeros(m1_ref.shape, F32)

    def work(s_new, m_new, s_old, m_old):
        q = q_ref[0, 0]
        m_prev = m_old[...]
        nc = s_new.shape[0] // MLA_KEY_CHUNK
        chunk = lambda c: slice(c * MLA_KEY_CHUNK, (c + 1) * MLA_KEY_CHUNK)

        def probs(c):
            return jnp.exp2(s_old[chunk(c), :] - m_prev).astype(BF16)

        def weigh(c, p, ot):
            part = jnp.dot(vt_ref[0, 0, :, chunk(c)], p, preferred_element_type=F32)
            return part if ot is None else ot + part

        def score(c, m_run):
            st = jnp.dot(k_ref[0, 0, chunk(c), :], q, preferred_element_type=F32)
            s_new[chunk(c), :] = st
            mc = jnp.max(st, axis=0, keepdims=True)
            return mc if m_run is None else jnp.maximum(m_run, mc)

        ot = weigh(0, probs(0), None)
        m_run = None
        for c in range(nc - 1):
            m_run = score(c, m_run)
            ot = weigh(c + 1, probs(c + 1), ot)
        o = ot[0:MLA_V] / ot[MLA_V:MLA_V + 1]
        o_ref[0] = o.T.astype(BF16)
        m_new[...] = score(nc - 1, m_run)

    @pl.when(step % 2 == 0)
    def _():
        work(s0_ref, m0_ref, s1_ref, m1_ref)

    @pl.when(step % 2 == 1)
    def _():
        work(s1_ref, m1_ref, s0_ref, m0_ref)


def _mla_attention(qt, k, vt, tq):
    b, nh, seq, slot = k.shape
    nq = seq // tq
    n = b * nh * nq

    def tile(t):
        return t // (nh * nq), (t // nq) % nh, t % nq

    def q_map(s):
        bi, h, i = tile(jnp.minimum(s, n - 1))
        return bi, h, 0, i

    def k_map(s):
        bi, h, _ = tile(jnp.minimum(s, n - 1))
        return bi, h, 0, 0

    def vt_map(s):
        bi, h, _ = tile(jnp.maximum(s - 1, 0))
        return bi, h, 0, 0

    def o_map(s):
        bi, h, i = tile(jnp.maximum(s - 1, 0))
        return bi, i, h

    return pl.pallas_call(
        _mla_attn_body,
        grid=(n + 1,),
        in_specs=[
            pl.BlockSpec((1, 1, slot, tq), q_map),
            pl.BlockSpec((1, 1, seq, slot), k_map),
            pl.BlockSpec((1, 1, MLA_VT_ROWS, seq), vt_map),
        ],
        out_specs=pl.BlockSpec((1, tq, MLA_V), o_map),
        out_shape=jax.ShapeDtypeStruct((b, seq, nh * MLA_V), BF16),
        scratch_shapes=[pltpu.VMEM((seq, tq), F32), pltpu.VMEM((seq, tq), F32),
                        pltpu.VMEM((1, tq), F32), pltpu.VMEM((1, tq), F32)],
        compiler_params=_params("arbitrary"),
        name="mla_attention",
    )(qt, k, vt)


def _swa_pair_perm(w_q_cols):
    lead = w_q_cols.shape[:-1]
    w = w_q_cols.reshape(lead + (2, 2, SWA_GROUP, SWA_HEAD_DIM))
    w = jnp.swapaxes(w, -3, -2)
    return w.reshape(lead + (SWA_HEADS * SWA_HEAD_DIM,))


def _rope_tables(positions):
    half = MLA_ROPE // 2
    inv = ROPE_THETA ** (-jnp.arange(0, MLA_ROPE, 2, dtype=F32) / MLA_ROPE)
    ang = positions.astype(F32)[:, None] * inv[None, :]
    cos, sin = jnp.cos(ang), jnp.sin(ang)
    z = jnp.zeros_like(cos)
    pad = jnp.zeros((positions.shape[0], V7X_LANES - MLA_ROPE), F32)
    c = jnp.concatenate([cos, cos, pad], axis=1)
    s1 = jnp.concatenate([-sin, z, pad], axis=1)
    s2 = jnp.concatenate([z, sin, pad], axis=1)
    return c, s1, s2, cos.T, sin.T


def _pad_lanes(v, n):
    return jnp.pad(v, ((0, 0), (0, n - v.shape[1])))


def kernel(x, positions, norm_mix_g, norm_ffn_g, pool_w, pool_scale, swa_w_qkv, swa_q_gain, swa_k_gain, swa_sinks, swa_w_o, mla_w_down, mla_q_a_gain, mla_kv_a_gain, mla_w_uq, mla_w_ukv, mla_qn_gain, mla_qr_gain, mla_kn_gain, mla_kr_gain, mla_w_o, ffn_w_in, ffn_conv_w, ffn_conv_b, ffn_w_out):
    b, seq, d = x.shape
    m = b * seq
    depth = norm_mix_g.shape[0]
    t = _tiles(m, seq)
    xf = x.reshape(m, d)

    w_gate = ffn_w_in[:, :, :ffn_w_out.shape[1]].astype(BF16)
    f = ffn_w_out.shape[1]
    cwb = jnp.concatenate([ffn_conv_w, ffn_conv_b[:, None, :], jnp.zeros((depth, 4, f), F32)], axis=1)
    pool_wb = pool_w.astype(BF16)
    pool_bands = jnp.asarray(_pool_bands(), dtype=BF16)

    nq = SWA_HEADS * SWA_HEAD_DIM
    pos_col = positions.reshape(seq, 1)
    pos_row = positions.reshape(1, seq)
    blk = np.kron(np.eye(4, dtype=np.float32), np.ones((SWA_HEAD_DIM, SWA_HEAD_DIM), np.float32))
    e_blk = jnp.asarray(blk, dtype=BF16)
    c_tab, s1_tab, s2_tab, cos_t, sin_t = _rope_tables(positions)

    for i in range(depth):
        kind = i % N_MIXERS
        j = i // N_MIXERS
        gain = norm_mix_g[i][None, :]
        if kind == 0:
            xf = _pool_layer(xf, gain, pool_bands, pool_wb, pool_scale[j][None, :], j, seq, t["pool_ts"])
        elif kind == 1:
            w = swa_w_qkv[j].astype(BF16)
            w_qkv = jnp.concatenate([_swa_pair_perm(w[:, :nq]), w[:, nq:]], axis=1)
            w_o = _swa_pair_perm(swa_w_o[j].astype(BF16).T).T
            qg = jnp.tile(swa_q_gain[j], 4)[None, :]
            kg = jnp.tile(swa_k_gain[j], 4)[None, :]
            q, k, v = _swa_qkv(xf, gain, w_qkv, qg, kg, e_blk, t["proj_tm"])
            o = _swa_attention(q.reshape(b, seq, -1), k.reshape(b, seq, -1), v.reshape(b, seq, -1),
                               pos_col, pos_row, swa_sinks[j], seq)
            xf = _proj_residual(o.reshape(m, -1), w_o, xf, t["res_tm"])
        else:
            n_lat = MLA_Q_RANK + MLA_KV_RANK
            w_down = _pad_lanes(mla_w_down[j].astype(BF16), n_lat + V7X_LANES)
            wq = mla_w_uq[j].astype(BF16).reshape(MLA_Q_RANK, MLA_HEADS, MLA_NOPE + MLA_ROPE)
            wq = jnp.pad(wq, ((0, 0), (0, 0), (0, MLA_SLOT - MLA_NOPE - MLA_ROPE)))
            w_uqt = wq.reshape(MLA_Q_RANK, MLA_HEADS * MLA_SLOT).T
            wkv = mla_w_ukv[j].astype(BF16).reshape(MLA_KV_RANK, MLA_HEADS, MLA_NOPE + MLA_V)
            w_uk = wkv[:, :, :MLA_NOPE].reshape(MLA_KV_RANK, MLA_HEADS * MLA_NOPE)
            w_uvt = wkv[:, :, MLA_NOPE:].reshape(MLA_KV_RANK, MLA_HEADS * MLA_V).T
            up_tm = t["up_tm"]
            q_scale = (MLA_NOPE + MLA_ROPE) ** -0.5 * LOG2E
            qn = jnp.broadcast_to((mla_qn_gain[j] * q_scale)[:, None], (MLA_NOPE, up_tm))
            qr = jnp.broadcast_to((mla_qr_gain[j] * q_scale)[:, None], (MLA_ROPE, up_tm))
            kr = _pad_lanes(mla_kr_gain[j][None, :], V7X_LANES)
            cq, ckv, kpe = _mla_down(xf, gain, w_down, mla_q_a_gain[j][None, :], mla_kv_a_gain[j][None, :], kr,
                                     c_tab, s1_tab, s2_tab, seq, t["proj_tm"])
            qt, k, vt = _mla_up(cq, ckv, kpe, w_uqt, w_uk, w_uvt, qn, qr, mla_kn_gain[j][None, :],
                                cos_t, sin_t, b, seq, up_tm)
            o = _mla_attention(qt, k, vt, t["attn_tq"])
            xf = _proj_residual(o.reshape(m, -1), mla_w_o[j].astype(BF16), xf, t["res_tm"])
        xf = _ffn_layer(xf, norm_ffn_g[i][None, :], w_gate, ffn_w_in, cwb, ffn_w_out, i, seq,
                        t["ffn_tm"], t["ffn_tf"])
    return xf.reshape(b, seq, d)
```

```python
import functools

import numpy as np
import jax
import jax.numpy as jnp
from jax import lax
from jax.experimental import pallas as pl
from jax.experimental.pallas import tpu as pltpu

F32 = jnp.float32
BF16 = jnp.bfloat16

EPS = 1e-6
LOG2E = 1.4426950408889634
N_MIXERS = 3
POOL_WINDOWS = (2, 4, 8, 16)
POOL_HALO = 16
POOL_SUB = 128
POOL_BAND_K = 256
SWA_HEADS = 32
SWA_KV_HEADS = 4
SWA_GROUP = SWA_HEADS // SWA_KV_HEADS
SWA_HEAD_DIM = 64
SWA_WINDOW = 128
SWA_BLOCK = 128
SWA_SPAN = SWA_BLOCK + 2 * SWA_WINDOW
MLA_HEADS = 16
MLA_NOPE = 128
MLA_ROPE = 64
MLA_V = 128
MLA_Q_RANK = 512
MLA_KV_RANK = 512
MLA_SLOT = 256
MLA_VT_ROWS = MLA_V + 16
MLA_KEY_CHUNK = 512
ROPE_THETA = 10000.0

V7X_LANES = 128
V7X_SUBLANES = 8
V7X_VMEM_BYTES = 64 * 1024 * 1024
V7X_VMEM_LIMIT_BYTES = 56 * 1024 * 1024


def _tiles(m, seq):
    def fit(t):
        while seq % t:
            t //= 2
        return t
    return dict(ffn_tm=fit(1024), ffn_tf=512, pool_ts=fit(512), proj_tm=fit(512), up_tm=fit(256),
                attn_tq=fit(512), res_tm=fit(512))


def _params(*sem, vmem=V7X_VMEM_LIMIT_BYTES):
    return pltpu.CompilerParams(dimension_semantics=sem, vmem_limit_bytes=vmem)


def _rms(x, g):
    ms = jnp.mean(x * x, axis=-1, keepdims=True)
    return x * lax.rsqrt(ms + EPS) * g


def _halo_specs(tm, m, d, nargs, rows=V7X_SUBLANES):
    r = tm // rows
    last = m // rows - 1
    if nargs == 1:
        prev = pl.BlockSpec((rows, d), lambda i: (jnp.maximum(i * r - 1, 0), 0))
        nxt = pl.BlockSpec((rows, d), lambda i: (jnp.minimum((i + 1) * r, last), 0))
    else:
        prev = pl.BlockSpec((rows, d), lambda i, j: (jnp.maximum(i * r - 1, 0), 0))
        nxt = pl.BlockSpec((rows, d), lambda i, j: (jnp.minimum((i + 1) * r, last), 0))
    return prev, nxt


def _ffn_norm_tile(xp_ref, x_ref, xn_ref, g_ref, hn_ref, first_row, *, tm, seq):
    g = g_ref[...]
    hn_ref[0:tm, :] = _rms(x_ref[...], g).astype(BF16)
    s0 = first_row % seq
    keep_prev = jnp.where(s0 != 0, 7, -1)
    keep_next = jnp.where(s0 + tm != seq, 8, -1)
    ext = jnp.concatenate([_rms(xp_ref[...], g), _rms(xn_ref[...], g)], axis=0)
    r = lax.broadcasted_iota(jnp.int32, (16, 1), 0)
    ext = jnp.where((r == keep_prev) | (r == keep_next), ext, 0.0)
    hn_ref[tm:tm + 16, :] = ext.astype(BF16)


def _ffn_body(xp_ref, x_ref, xn_ref, g_ref, wg_ref, wv_ref, cwb_ref, wo_ref, o_ref, hn_ref, *, tm, seq):
    i = pl.program_id(0)
    j = pl.program_id(1)

    @pl.when(j == 0)
    def _():
        _ffn_norm_tile(xp_ref, x_ref, xn_ref, g_ref, hn_ref, i * tm, tm=tm, seq=seq)

    ug = jnp.dot(hn_ref[...], wg_ref[...].astype(BF16), preferred_element_type=F32)
    uv = jnp.dot(hn_ref[0:tm, :], wv_ref[...].astype(BF16), preferred_element_type=F32)
    gm = ug[0:tm]
    g_prev = ug[tm + 7:tm + 8]
    g_next = ug[tm + 8:tm + 9]
    rows = lax.broadcasted_iota(jnp.int32, (tm, 1), 0)
    up = jnp.where(rows == 0, g_prev, pltpu.roll(gm, 1, 0))
    dn = jnp.where(rows == tm - 1, g_next, pltpu.roll(gm, tm - 1, 0))
    cwb = cwb_ref[...]
    gc = up * cwb[0:1] + gm * cwb[1:2] + dn * cwb[2:3] + cwb[3:4]
    act = (gc * jax.nn.sigmoid(gc) * uv).astype(BF16)
    acc = jnp.where(j == 0, x_ref[...], o_ref[...])
    o_ref[...] = acc + jnp.dot(act, wo_ref[...].astype(BF16), preferred_element_type=F32)


def _ffn_layer(x, gain, w_in, cwb, w_out, layer, seq, tm, tf):
    m, d = x.shape
    f = w_out.shape[1]
    nf = f // tf
    prev, nxt = _halo_specs(tm, m, d, 2)
    return pl.pallas_call(
        functools.partial(_ffn_body, tm=tm, seq=seq),
        grid=(m // tm, nf),
        in_specs=[
            prev,
            pl.BlockSpec((tm, d), lambda i, j: (i, 0)),
            nxt,
            pl.BlockSpec((1, d), lambda i, j: (0, 0)),
            pl.BlockSpec((None, d, tf), lambda i, j: (layer, 0, j)),
            pl.BlockSpec((None, d, tf), lambda i, j: (layer, 0, j + nf)),
            pl.BlockSpec((None, 8, tf), lambda i, j: (layer, 0, j)),
            pl.BlockSpec((None, tf, d), lambda i, j: (layer, j, 0)),
        ],
        out_specs=pl.BlockSpec((tm, d), lambda i, j: (i, 0)),
        out_shape=jax.ShapeDtypeStruct((m, d), F32),
        scratch_shapes=[pltpu.VMEM((tm + 16, d), BF16)],
        compiler_params=_params("arbitrary", "arbitrary", vmem=V7X_VMEM_BYTES),
        name="ffn_convglu",
    )(x, x, x, gain, w_in, w_in, cwb, w_out)


def _pool_bands():
    t = np.arange(POOL_SUB)[:, None]
    s = np.arange(POOL_BAND_K)[None, :]
    bands = [((s >= t + POOL_HALO - w // 2) & (s <= t + POOL_HALO + (w - 1 - w // 2))) for w in POOL_WINDOWS]
    return np.stack(bands).astype(np.float32)


def _pool_body(xp_ref, x_ref, xn_ref, g_ref, band_ref, w_ref, sc_ref, o_ref, hb_ref, *, ts, seq):
    i = pl.program_id(0)
    data_rows = ts + 2 * POOL_HALO

    @pl.when(i == 0)
    def _():
        hb_ref[data_rows:, :] = jnp.zeros((hb_ref.shape[0] - data_rows, hb_ref.shape[1]), BF16)

    g = g_ref[...]
    x = x_ref[...]
    s0 = (i * ts) % seq
    h = _rms(x, g)

    hb_ref[0:POOL_HALO, :] = (_rms(xp_ref[...], g) * jnp.where(s0 != 0, 1.0, 0.0)).astype(BF16)
    hb_ref[POOL_HALO:POOL_HALO + ts, :] = h.astype(BF16)
    hb_ref[POOL_HALO + ts:data_rows, :] = (_rms(xn_ref[...], g) * jnp.where(s0 + ts != seq, 1.0, 0.0)).astype(BF16)

    pos = s0 + lax.broadcasted_iota(jnp.int32, (ts, 1), 0)
    dg = x.shape[1] // len(POOL_WINDOWS)
    for gi, w in enumerate(POOL_WINDOWS):
        left = w // 2
        right = w - 1 - left
        cols = slice(gi * dg, (gi + 1) * dg)
        band = band_ref[gi]
        wsum = []
        for r0 in range(0, ts, POOL_SUB):
            rows = slice(r0, r0 + POOL_BAND_K)
            wsum.append(jnp.dot(band, hb_ref[rows, cols], preferred_element_type=F32))
        wsum = jnp.concatenate(wsum, axis=0)
        cnt = (jnp.minimum(pos + right + 1, seq) - jnp.maximum(pos - left, 0)).astype(F32)
        pooled = (wsum * (1.0 / cnt) - h[:, cols]).astype(BF16)
        y = jnp.dot(pooled, w_ref[gi], preferred_element_type=F32) * sc_ref[:, cols]
        o_ref[:, cols] = x[:, cols] + y


def _pool_layer(x, gain, bands, pool_w, scale, layer, seq, ts):
    m, d = x.shape
    ng, dg = pool_w.shape[1], pool_w.shape[2]
    prev, nxt = _halo_specs(ts, m, d, 1, POOL_HALO)
    scratch = pltpu.VMEM((ts + POOL_BAND_K - POOL_SUB, d), BF16)
    return pl.pallas_call(
        functools.partial(_pool_body, ts=ts, seq=seq),
        grid=(m // ts,),
        in_specs=[
            prev,
            pl.BlockSpec((ts, d), lambda i: (i, 0)),
            nxt,
            pl.BlockSpec((1, d), lambda i: (0, 0)),
            pl.BlockSpec(bands.shape, lambda i: (0, 0, 0)),
            pl.BlockSpec((None, ng, dg, dg), lambda i: (layer, 0, 0, 0)),
            pl.BlockSpec((1, d), lambda i: (0, 0)),
        ],
        out_specs=pl.BlockSpec((ts, d), lambda i: (i, 0)),
        out_shape=jax.ShapeDtypeStruct((m, d), F32),
        scratch_shapes=[scratch],
        compiler_params=_params("arbitrary"),
        name="pool_mixer",
    )(x, x, x, gain, bands, pool_w, scale)


def _proj_res_body(a_ref, w_ref, x_ref, o_ref):
    o_ref[...] = x_ref[...] + jnp.dot(a_ref[...], w_ref[...], preferred_element_type=F32)


def _proj_residual(a, w, x, tm):
    m, d = x.shape
    k = a.shape[1]
    return pl.pallas_call(
        _proj_res_body,
        grid=(m // tm,),
        in_specs=[
            pl.BlockSpec((tm, k), lambda i: (i, 0)),
            pl.BlockSpec((k, d), lambda i: (0, 0)),
            pl.BlockSpec((tm, d), lambda i: (i, 0)),
        ],
        out_specs=pl.BlockSpec((tm, d), lambda i: (i, 0)),
        out_shape=jax.ShapeDtypeStruct((m, d), F32),
        compiler_params=_params("arbitrary"),
        name="proj_residual",
    )(a, w, x)


def _group_sumsq(z, e):
    return jnp.dot((z * z).astype(BF16), e, preferred_element_type=F32)


def _swa_qkv_body(x_ref, g_ref, w_ref, qg_ref, kg_ref, e_ref, q_ref, k_ref, v_ref):
    hn = _rms(x_ref[...], g_ref[...]).astype(BF16)
    qkv = jnp.dot(hn, w_ref[...], preferred_element_type=F32)
    e = e_ref[...]
    nq = q_ref.shape[1]
    nkv = k_ref.shape[1]
    cw = e.shape[0]
    inv_hd = 1.0 / SWA_HEAD_DIM

    def head_norm(z, gain):
        return z * lax.rsqrt(_group_sumsq(z, e) * inv_hd + EPS) * gain

    qg = qg_ref[...] * (SWA_HEAD_DIM ** -0.5 * LOG2E)
    for c in range(nq // cw):
        cols = slice(c * cw, (c + 1) * cw)
        q_ref[:, cols] = head_norm(qkv[:, cols], qg).astype(BF16)
    k_ref[...] = head_norm(qkv[:, nq:nq + nkv], kg_ref[...]).astype(BF16)
    v_ref[...] = qkv[:, nq + nkv:].astype(BF16)


def _swa_qkv(x, gain, w_qkv, qg, kg, e, tm):
    m, d = x.shape
    nkv = SWA_KV_HEADS * SWA_HEAD_DIM
    nq = w_qkv.shape[1] - 2 * nkv
    return pl.pallas_call(
        _swa_qkv_body,
        grid=(m // tm,),
        in_specs=[
            pl.BlockSpec((tm, d), lambda i: (i, 0)),
            pl.BlockSpec((1, d), lambda i: (0, 0)),
            pl.BlockSpec(w_qkv.shape, lambda i: (0, 0)),
            pl.BlockSpec((1, nkv), lambda i: (0, 0)),
            pl.BlockSpec((1, nkv), lambda i: (0, 0)),
            pl.BlockSpec(e.shape, lambda i: (0, 0)),
        ],
        out_specs=[
            pl.BlockSpec((tm, nq), lambda i: (i, 0)),
            pl.BlockSpec((tm, nkv), lambda i: (i, 0)),
            pl.BlockSpec((tm, nkv), lambda i: (i, 0)),
        ],
        out_shape=[
            jax.ShapeDtypeStruct((m, nq), BF16),
            jax.ShapeDtypeStruct((m, nkv), BF16),
            jax.ShapeDtypeStruct((m, nkv), BF16),
        ],
        compiler_params=_params("arbitrary"),
        name="swa_qkv",
    )(x, gain, w_qkv, qg, kg, e)


def _swa_slopes():
    return [float(2.0 ** (-8.0 * (i + 1) / SWA_HEADS)) for i in range(SWA_HEADS)]


def _swa_attn_body(sink_ref, q_ref, k_ref, v_ref, pc_ref, pr_ref, o_ref, *, seq):
    j = pl.program_id(1)
    hd = SWA_HEAD_DIM
    start = pl.multiple_of(jnp.clip(j * SWA_BLOCK - SWA_WINDOW, 0, seq - SWA_SPAN), SWA_BLOCK)
    kblk = k_ref[0, pl.ds(start, SWA_SPAN), :]
    vblk = v_ref[0, pl.ds(start, SWA_SPAN), :]
    pk = pr_ref[:, pl.ds(start, SWA_SPAN)]
    pq = pc_ref[...]
    qi = j * SWA_BLOCK + lax.broadcasted_iota(jnp.int32, (SWA_BLOCK, 1), 0)
    ki = start + lax.broadcasted_iota(jnp.int32, (1, SWA_SPAN), 1)
    in_win = jnp.abs(qi - ki) <= SWA_WINDOW
    dist = jnp.where(in_win, jnp.abs(pq - pk).astype(F32), jnp.inf)
    lane = lax.broadcasted_iota(jnp.int32, (1, 2 * hd), 1)
    slopes = _swa_slopes()
    zero = jnp.zeros((), BF16)
    lane_full = lax.broadcasted_iota(jnp.int32, (SWA_SPAN, 2 * hd), 1)
    ind_a = jnp.where(lane_full < hd, 1.0, 0.0).astype(BF16)
    ind_b = jnp.where(lane_full >= hd, 1.0, 0.0).astype(BF16)
    def pair_cols(mp):
        return [slice((mp * SWA_GROUP + g) * 2 * hd, (mp * SWA_GROUP + g + 1) * 2 * hd) for g in range(SWA_GROUP)]

    def score(mp):
        kp = kblk[:, mp * 2 * hd:(mp + 1) * 2 * hd]
        k2 = jnp.concatenate([jnp.where(lane < hd, kp, zero), jnp.where(lane >= hd, kp, zero)], axis=0)
        q_all = jnp.concatenate([q_ref[0, :, cols] for cols in pair_cols(mp)], axis=0)
        return lax.dot_general(q_all, k2, (((1,), (1,)), ((), ())), preferred_element_type=F32)

    def softmax(mp, s_all):
        ps, sink_terms = [], []
        for g in range(SWA_GROUP):
            ha = (2 * mp) * SWA_GROUP + g
            hb = (2 * mp + 1) * SWA_GROUP + g
            s = s_all[g * SWA_BLOCK:(g + 1) * SWA_BLOCK]
            sa = s[:, :SWA_SPAN] - (LOG2E * slopes[ha]) * dist
            sb = s[:, SWA_SPAN:] - (LOG2E * slopes[hb]) * dist
            sink_a = sink_ref[ha] * LOG2E
            sink_b = sink_ref[hb] * LOG2E
            ma = jnp.maximum(jnp.max(sa, axis=-1, keepdims=True), sink_a)
            mb = jnp.maximum(jnp.max(sb, axis=-1, keepdims=True), sink_b)
            ps.append(jnp.concatenate([jnp.exp2(sa - ma), jnp.exp2(sb - mb)], axis=1).astype(BF16))
            sink_terms.append(jnp.where(lane < hd, jnp.exp2(sink_a - ma), jnp.exp2(sink_b - mb)))
        return jnp.concatenate(ps, axis=0), sink_terms

    def weigh(mp, p_all, sink_terms):
        vp = vblk[:, mp * 2 * hd:(mp + 1) * 2 * hd]
        v2 = jnp.concatenate([jnp.concatenate([jnp.where(lane < hd, vp, zero), ind_a], axis=1),
                              jnp.concatenate([jnp.where(lane >= hd, vp, zero), ind_b], axis=1)], axis=0)
        o_all = jnp.dot(p_all, v2, preferred_element_type=F32)
        for g, cols in enumerate(pair_cols(mp)):
            og = o_all[g * SWA_BLOCK:(g + 1) * SWA_BLOCK]
            o_ref[0, :, cols] = (og[:, :2 * hd] / (og[:, 2 * hd:] + sink_terms[g])).astype(BF16)

    for mp in range(SWA_KV_HEADS // 2):
        weigh(mp, *softmax(mp, score(mp)))


def _swa_attention(q, k, v, pos_col, pos_row, sinks, seq):
    b = q.shape[0]
    nq = q.shape[2]
    nkv = k.shape[2]
    return pl.pallas_call(
        functools.partial(_swa_attn_body, seq=seq),
        grid=(b, seq // SWA_BLOCK),
        in_specs=[
            pl.BlockSpec(memory_space=pltpu.SMEM),
            pl.BlockSpec((1, SWA_BLOCK, nq), lambda bi, j: (bi, j, 0)),
            pl.BlockSpec((1, seq, nkv), lambda bi, j: (bi, 0, 0)),
            pl.BlockSpec((1, seq, nkv), lambda bi, j: (bi, 0, 0)),
            pl.BlockSpec((SWA_BLOCK, 1), lambda bi, j: (j, 0)),
            pl.BlockSpec((1, seq), lambda bi, j: (0, 0)),
        ],
        out_specs=pl.BlockSpec((1, SWA_BLOCK, nq), lambda bi, j: (bi, j, 0)),
        out_shape=jax.ShapeDtypeStruct(q.shape, BF16),
        compiler_params=_params("arbitrary", "arbitrary"),
        name="swa_attention",
    )(sinks, q, k, v, pos_col, pos_row)


def _rope64(z, c, s1, s2):
    return z * c + pltpu.roll(z, V7X_LANES - MLA_ROPE // 2, 1) * s1 + pltpu.roll(z, MLA_ROPE // 2, 1) * s2


def _slot_rms(z, gain):
    ms = jnp.sum(z * z, axis=-1, keepdims=True) * (1.0 / MLA_ROPE)
    return z * lax.rsqrt(ms + EPS) * gain


def _mla_down_body(x_ref, g_ref, w_ref, qa_ref, kva_ref, kr_ref, c_ref, s1_ref, s2_ref,
                   cq_ref, ckv_ref, kpe_ref):
    hn = _rms(x_ref[...], g_ref[...]).astype(BF16)
    d = jnp.dot(hn, w_ref[...], preferred_element_type=F32)
    cq_ref[...] = _rms(d[:, :MLA_Q_RANK], qa_ref[...]).astype(BF16)
    ckv_ref[...] = _rms(d[:, MLA_Q_RANK:MLA_Q_RANK + MLA_KV_RANK], kva_ref[...]).astype(BF16)
    kp = _slot_rms(d[:, MLA_Q_RANK + MLA_KV_RANK:], kr_ref[...])
    kpe_ref[...] = _rope64(kp, c_ref[...], s1_ref[...], s2_ref[...]).astype(BF16)


def _mla_down(x, gain, w_down, qa, kva, kr, c, s1, s2, seq, tm):
    m, d = x.shape
    nt = seq // tm
    tab = pl.BlockSpec((tm, V7X_LANES), lambda i: (i % nt, 0))
    row = lambda n: pl.BlockSpec((1, n), lambda i: (0, 0))
    return pl.pallas_call(
        _mla_down_body,
        grid=(m // tm,),
        in_specs=[
            pl.BlockSpec((tm, d), lambda i: (i, 0)),
            row(d),
            pl.BlockSpec(w_down.shape, lambda i: (0, 0)),
            row(MLA_Q_RANK), row(MLA_KV_RANK), row(V7X_LANES),
            tab, tab, tab,
        ],
        out_specs=[
            pl.BlockSpec((tm, MLA_Q_RANK), lambda i: (i, 0)),
            pl.BlockSpec((tm, MLA_KV_RANK), lambda i: (i, 0)),
            pl.BlockSpec((tm, V7X_LANES), lambda i: (i, 0)),
        ],
        out_shape=[
            jax.ShapeDtypeStruct((m, MLA_Q_RANK), BF16),
            jax.ShapeDtypeStruct((m, MLA_KV_RANK), BF16),
            jax.ShapeDtypeStruct((m, V7X_LANES), BF16),
        ],
        compiler_params=_params("arbitrary"),
        name="mla_down",
    )(x, gain, w_down, qa, kva, kr, c, s1, s2)


def _mla_up_body(cq_ref, ckv_ref, kpe_ref, wuqt_ref, wuk_ref, wuvt_ref, qn_ref, qr_ref, kn_ref,
                 cos_ref, sin_ref, qt_ref, k_ref, vt_ref):
    ckv = ckv_ref[...]
    nt = (((1,), (1,)), ((), ()))
    qt = lax.dot_general(wuqt_ref[...], cq_ref[...], nt, preferred_element_type=F32)
    vt = lax.dot_general(wuvt_ref[...], ckv, nt, preferred_element_type=F32)
    kn = jnp.dot(ckv, wuk_ref[...], preferred_element_type=F32)
    kpe = kpe_ref[...]
    qng = qn_ref[...]
    qrg = qr_ref[...]
    kng = kn_ref[...]
    cos, sin = cos_ref[...], sin_ref[...]
    tm = ckv.shape[0]
    half = MLA_ROPE // 2
    ones = jnp.ones((MLA_VT_ROWS - MLA_V, tm), BF16)
    zeros = jnp.zeros((MLA_SLOT - MLA_NOPE - MLA_ROPE, tm), BF16)
    for h in range(MLA_HEADS):
        lo = h * MLA_SLOT
        mid = lo + MLA_NOPE
        qn = qt[lo:mid]
        ms = jnp.mean(qn * qn, axis=0, keepdims=True)
        qt_ref[0, h, 0:MLA_NOPE, :] = (qn * lax.rsqrt(ms + EPS) * qng).astype(BF16)
        x1 = qt[mid:mid + half]
        x2 = qt[mid + half:mid + 2 * half]
        ms = (jnp.sum(x1 * x1, axis=0, keepdims=True)
              + jnp.sum(x2 * x2, axis=0, keepdims=True)) * (1.0 / MLA_ROPE)
        r = lax.rsqrt(ms + EPS)
        x1 = x1 * r * qrg[0:half]
        x2 = x2 * r * qrg[half:2 * half]
        qt_ref[0, h, MLA_NOPE:MLA_NOPE + half, :] = (x1 * cos - x2 * sin).astype(BF16)
        qt_ref[0, h, MLA_NOPE + half:MLA_NOPE + 2 * half, :] = (x2 * cos + x1 * sin).astype(BF16)
        qt_ref[0, h, MLA_NOPE + 2 * half:MLA_SLOT, :] = zeros
        k_ref[0, h, :, 0:MLA_NOPE] = _rms(kn[:, h * MLA_NOPE:(h + 1) * MLA_NOPE], kng).astype(BF16)
        k_ref[0, h, :, MLA_NOPE:MLA_SLOT] = kpe
        vt_ref[0, h, 0:MLA_V, :] = vt[h * MLA_V:(h + 1) * MLA_V, :].astype(BF16)
        vt_ref[0, h, MLA_V:MLA_VT_ROWS, :] = ones


def _mla_up(cq, ckv, kpe, w_uqt, w_uk, w_uvt, qn, qr, kn, cos_t, sin_t, b, seq, tm):
    nt = seq // tm
    tok = lambda n: pl.BlockSpec((tm, n), lambda bi, i: (bi * nt + i, 0))
    full = lambda a: pl.BlockSpec(a.shape, lambda bi, i: (0, 0))
    tab = pl.BlockSpec((MLA_ROPE // 2, tm), lambda bi, i: (0, i))
    out = pl.BlockSpec((1, MLA_HEADS, tm, MLA_SLOT), lambda bi, i: (bi, 0, i, 0))
    shape = jax.ShapeDtypeStruct((b, MLA_HEADS, seq, MLA_SLOT), BF16)
    out_q = pl.BlockSpec((1, MLA_HEADS, MLA_SLOT, tm), lambda bi, i: (bi, 0, 0, i))
    shape_q = jax.ShapeDtypeStruct((b, MLA_HEADS, MLA_SLOT, seq), BF16)
    out_v = pl.BlockSpec((1, MLA_HEADS, MLA_VT_ROWS, tm), lambda bi, i: (bi, 0, 0, i))
    shape_v = jax.ShapeDtypeStruct((b, MLA_HEADS, MLA_VT_ROWS, seq), BF16)
    return pl.pallas_call(
        _mla_up_body,
        grid=(b, nt),
        in_specs=[tok(MLA_Q_RANK), tok(MLA_KV_RANK), tok(V7X_LANES), full(w_uqt), full(w_uk), full(w_uvt),
                  full(qn), full(qr), full(kn), tab, tab],
        out_specs=[out_q, out, out_v],
        out_shape=[shape_q, shape, shape_v],
        compiler_params=_params("arbitrary", "arbitrary"),
        name="mla_up",
    )(cq, ckv, kpe, w_uqt, w_uk, w_uvt, qn, qr, kn, cos_t, sin_t)


def _mla_attn_body(q_ref, k_ref, vt_ref, o_ref, s0_ref, s1_ref, m0_ref, m1_ref):
    step = pl.program_id(0)

    @pl.when(step == 0)
    def _():
        s1_ref[...] = jnp.zeros(s1_ref.shape, F32)
        m1_ref[...] = jnp.zeros(m1_ref.shape, F32)

    def work(s_new, m_new, s_old, m_old):
        q = q_ref[0, 0]
        m_prev = m_old[...]
        nc = s_new.shape[0] // MLA_KEY_CHUNK
        chunk = lambda c: slice(c * MLA_KEY_CHUNK, (c + 1) * MLA_KEY_CHUNK)

        def probs(c):
            return jnp.exp2(s_old[chunk(c), :] - m_prev).astype(BF16)

        def weigh(c, p, ot):
            part = jnp.dot(vt_ref[0, 0, :, chunk(c)], p, preferred_element_type=F32)
            return part if ot is None else ot + part

        def score(c, m_run):
            st = jnp.dot(k_ref[0, 0, chunk(c), :], q, preferred_element_type=F32)
            s_new[chunk(c), :] = st
            mc = jnp.max(st, axis=0, keepdims=True)
            return mc if m_run is None else jnp.maximum(m_run, mc)

        ot = weigh(0, probs(0), None)
        m_run = None
        for c in range(nc - 1):
            m_run = score(c, m_run)
            ot = weigh(c + 1, probs(c + 1), ot)
        o = ot[0:MLA_V] / ot[MLA_V:MLA_V + 1]
        o_ref[0] = o.T.astype(BF16)
        m_new[...] = score(nc - 1, m_run)

    @pl.when(step % 2 == 0)
    def _():
        work(s0_ref, m0_ref, s1_ref, m1_ref)

    @pl.when(step % 2 == 1)
    def _():
        work(s1_ref, m1_ref, s0_ref, m0_ref)


def _mla_attention(qt, k, vt, tq):
    b, nh, seq, slot = k.shape
    nq = seq // tq
    n = b * nh * nq

    def tile(t):
        return t // (nh * nq), (t // nq) % nh, t % nq

    def q_map(s):
        bi, h, i = tile(jnp.minimum(s, n - 1))
        return bi, h, 0, i

    def k_map(s):
        bi, h, _ = tile(jnp.minimum(s, n - 1))
        return bi, h, 0, 0

    def vt_map(s):
        bi, h, _ = tile(jnp.maximum(s - 1, 0))
        return bi, h, 0, 0

    def o_map(s):
        bi, h, i = tile(jnp.maximum(s - 1, 0))
        return bi, i, h

    return pl.pallas_call(
        _mla_attn_body,
        grid=(n + 1,),
        in_specs=[
            pl.BlockSpec((1, 1, slot, tq), q_map),
            pl.BlockSpec((1, 1, seq, slot), k_map),
            pl.BlockSpec((1, 1, MLA_VT_ROWS, seq), vt_map),
        ],
        out_specs=pl.BlockSpec((1, tq, MLA_V), o_map),
        out_shape=jax.ShapeDtypeStruct((b, seq, nh * MLA_V), BF16),
        scratch_shapes=[pltpu.VMEM((seq, tq), F32), pltpu.VMEM((seq, tq), F32),
                        pltpu.VMEM((1, tq), F32), pltpu.VMEM((1, tq), F32)],
        compiler_params=_params("arbitrary"),
        name="mla_attention",
    )(qt, k, vt)


def _swa_pair_perm(w_q_cols):
    lead = w_q_cols.shape[:-1]
    w = w_q_cols.reshape(lead + (2, 2, SWA_GROUP, SWA_HEAD_DIM))
    w = jnp.swapaxes(w, -3, -2)
    return w.reshape(lead + (SWA_HEADS * SWA_HEAD_DIM,))


def _rope_tables(positions):
    half = MLA_ROPE // 2
    inv = ROPE_THETA ** (-jnp.arange(0, MLA_ROPE, 2, dtype=F32) / MLA_ROPE)
    ang = positions.astype(F32)[:, None] * inv[None, :]
    cos, sin = jnp.cos(ang), jnp.sin(ang)
    z = jnp.zeros_like(cos)
    pad = jnp.zeros((positions.shape[0], V7X_LANES - MLA_ROPE), F32)
    c = jnp.concatenate([cos, cos, pad], axis=1)
    s1 = jnp.concatenate([-sin, z, pad], axis=1)
    s2 = jnp.concatenate([z, sin, pad], axis=1)
    return c, s1, s2, cos.T, sin.T


def _pad_lanes(v, n):
    return jnp.pad(v, ((0, 0), (0, n - v.shape[1])))


def kernel(x, positions, norm_mix_g, norm_ffn_g, pool_w, pool_scale, swa_w_qkv, swa_q_gain, swa_k_gain, swa_sinks, swa_w_o, mla_w_down, mla_q_a_gain, mla_kv_a_gain, mla_w_uq, mla_w_ukv, mla_qn_gain, mla_qr_gain, mla_kn_gain, mla_kr_gain, mla_w_o, ffn_w_in, ffn_conv_w, ffn_conv_b, ffn_w_out):
    b, seq, d = x.shape
    m = b * seq
    depth = norm_mix_g.shape[0]
    t = _tiles(m, seq)
    xf = x.reshape(m, d)

    f = ffn_w_out.shape[1]
    cwb = jnp.concatenate([ffn_conv_w, ffn_conv_b[:, None, :], jnp.zeros((depth, 4, f), F32)], axis=1)
    pool_wb = pool_w.astype(BF16)
    pool_bands = jnp.asarray(_pool_bands(), dtype=BF16)

    nq = SWA_HEADS * SWA_HEAD_DIM
    pos_col = positions.reshape(seq, 1)
    pos_row = positions.reshape(1, seq)
    blk = np.kron(np.eye(4, dtype=np.float32), np.ones((SWA_HEAD_DIM, SWA_HEAD_DIM), np.float32))
    e_blk = jnp.asarray(blk, dtype=BF16)
    c_tab, s1_tab, s2_tab, cos_t, sin_t = _rope_tables(positions)

    for i in range(depth):
        kind = i % N_MIXERS
        j = i // N_MIXERS
        gain = norm_mix_g[i][None, :]
        if kind == 0:
            xf = _pool_layer(xf, gain, pool_bands, pool_wb, pool_scale[j][None, :], j, seq, t["pool_ts"])
        elif kind == 1:
            w = swa_w_qkv[j].astype(BF16)
            w_qkv = jnp.concatenate([_swa_pair_perm(w[:, :nq]), w[:, nq:]], axis=1)
            w_o = _swa_pair_perm(swa_w_o[j].astype(BF16).T).T
            qg = jnp.tile(swa_q_gain[j], 4)[None, :]
            kg = jnp.tile(swa_k_gain[j], 4)[None, :]
            q, k, v = _swa_qkv(xf, gain, w_qkv, qg, kg, e_blk, t["proj_tm"])
            o = _swa_attention(q.reshape(b, seq, -1), k.reshape(b, seq, -1), v.reshape(b, seq, -1),
                               pos_col, pos_row, swa_sinks[j], seq)
            xf = _proj_residual(o.reshape(m, -1), w_o, xf, t["res_tm"])
        else:
            n_lat = MLA_Q_RANK + MLA_KV_RANK
            w_down = _pad_lanes(mla_w_down[j].astype(BF16), n_lat + V7X_LANES)
            wq = mla_w_uq[j].astype(BF16).reshape(MLA_Q_RANK, MLA_HEADS, MLA_NOPE + MLA_ROPE)
            wq = jnp.pad(wq, ((0, 0), (0, 0), (0, MLA_SLOT - MLA_NOPE - MLA_ROPE)))
            w_uqt = wq.reshape(MLA_Q_RANK, MLA_HEADS * MLA_SLOT).T
            wkv = mla_w_ukv[j].astype(BF16).reshape(MLA_KV_RANK, MLA_HEADS, MLA_NOPE + MLA_V)
            w_uk = wkv[:, :, :MLA_NOPE].reshape(MLA_KV_RANK, MLA_HEADS * MLA_NOPE)
            w_uvt = wkv[:, :, MLA_NOPE:].reshape(MLA_KV_RANK, MLA_HEADS * MLA_V).T
            up_tm = t["up_tm"]
            q_scale = (MLA_NOPE + MLA_ROPE) ** -0.5 * LOG2E
            qn = jnp.broadcast_to((mla_qn_gain[j] * q_scale)[:, None], (MLA_NOPE, up_tm))
            qr = jnp.broadcast_to((mla_qr_gain[j] * q_scale)[:, None], (MLA_ROPE, up_tm))
            kr = _pad_lanes(mla_kr_gain[j][None, :], V7X_LANES)
            cq, ckv, kpe = _mla_down(xf, gain, w_down, mla_q_a_gain[j][None, :], mla_kv_a_gain[j][None, :], kr,
                                     c_tab, s1_tab, s2_tab, seq, t["proj_tm"])
            qt, k, vt = _mla_up(cq, ckv, kpe, w_uqt, w_uk, w_uvt, qn, qr, mla_kn_gain[j][None, :],
                                cos_t, sin_t, b, seq, up_tm)
            o = _mla_attention(qt, k, vt, t["attn_tq"])
            xf = _proj_residual(o.reshape(m, -1), mla_w_o[j].astype(BF16), xf, t["res_tm"])
        xf = _ffn_layer(xf, norm_ffn_g[i][None, :], ffn_w_in, cwb, ffn_w_out, i, seq, t["ffn_tm"], t["ffn_tf"])
    return xf.reshape(b, seq, d)
```

```python
import functools

import numpy as np
import jax
import jax.numpy as jnp
from jax import lax
from jax.experimental import pallas as pl
from jax.experimental.pallas import tpu as pltpu

F32 = jnp.float32
BF16 = jnp.bfloat16

EPS = 1e-6
LOG2E = 1.4426950408889634
N_MIXERS = 3
POOL_WINDOWS = (2, 4, 8, 16)
POOL_HALO = 16
POOL_SUB = 128
POOL_BAND_K = 256
SWA_HEADS = 32
SWA_KV_HEADS = 4
SWA_GROUP = SWA_HEADS // SWA_KV_HEADS
SWA_HEAD_DIM = 64
SWA_WINDOW = 128
SWA_BLOCK = 128
SWA_SPAN = SWA_BLOCK + 2 * SWA_WINDOW
MLA_HEADS = 16
MLA_NOPE = 128
MLA_ROPE = 64
MLA_V = 128
MLA_Q_RANK = 512
MLA_KV_RANK = 512
MLA_SLOT = 256
MLA_VT_ROWS = MLA_V + 16
MLA_KEY_CHUNK = 512
MLA_SUB_TILE = 512
ROPE_THETA = 10000.0

V7X_LANES = 128
V7X_SUBLANES = 8
V7X_VMEM_BYTES = 64 * 1024 * 1024
V7X_VMEM_LIMIT_BYTES = 56 * 1024 * 1024


def _tiles(m, seq):
    def fit(t):
        while seq % t:
            t //= 2
        return t
    return dict(ffn_tm=fit(1024), ffn_tf=512, pool_ts=fit(512), proj_tm=fit(512), up_tm=fit(256),
                attn_tq=fit(1024), res_tm=fit(512))


def _params(*sem, vmem=V7X_VMEM_LIMIT_BYTES):
    return pltpu.CompilerParams(dimension_semantics=sem, vmem_limit_bytes=vmem)


def _rms(x, g):
    ms = jnp.mean(x * x, axis=-1, keepdims=True)
    return x * lax.rsqrt(ms + EPS) * g


def _halo_specs(tm, m, d, nargs, rows=V7X_SUBLANES):
    r = tm // rows
    last = m // rows - 1
    if nargs == 1:
        prev = pl.BlockSpec((rows, d), lambda i: (jnp.maximum(i * r - 1, 0), 0))
        nxt = pl.BlockSpec((rows, d), lambda i: (jnp.minimum((i + 1) * r, last), 0))
    else:
        prev = pl.BlockSpec((rows, d), lambda i, j: (jnp.maximum(i * r - 1, 0), 0))
        nxt = pl.BlockSpec((rows, d), lambda i, j: (jnp.minimum((i + 1) * r, last), 0))
    return prev, nxt


def _ffn_norm_tile(xp_ref, x_ref, xn_ref, g_ref, hn_ref, first_row, *, tm, seq):
    g = g_ref[...]
    hn_ref[0:tm, :] = _rms(x_ref[...], g).astype(BF16)
    s0 = first_row % seq
    keep_prev = jnp.where(s0 != 0, 7, -1)
    keep_next = jnp.where(s0 + tm != seq, 8, -1)
    ext = jnp.concatenate([_rms(xp_ref[...], g), _rms(xn_ref[...], g)], axis=0)
    r = lax.broadcasted_iota(jnp.int32, (16, 1), 0)
    ext = jnp.where((r == keep_prev) | (r == keep_next), ext, 0.0)
    hn_ref[tm:tm + 16, :] = ext.astype(BF16)


def _ffn_body(xp_ref, x_ref, xn_ref, g_ref, wg_ref, wv_ref, cwb_ref, wo_ref, o_ref, hn_ref, *, tm, seq):
    i = pl.program_id(0)
    j = pl.program_id(1)

    @pl.when(j == 0)
    def _():
        _ffn_norm_tile(xp_ref, x_ref, xn_ref, g_ref, hn_ref, i * tm, tm=tm, seq=seq)

    ug = jnp.dot(hn_ref[...], wg_ref[...].astype(BF16), preferred_element_type=F32)
    uv = jnp.dot(hn_ref[0:tm, :], wv_ref[...].astype(BF16), preferred_element_type=F32)
    gm = ug[0:tm]
    g_prev = ug[tm + 7:tm + 8]
    g_next = ug[tm + 8:tm + 9]
    rows = lax.broadcasted_iota(jnp.int32, (tm, 1), 0)
    up = jnp.where(rows == 0, g_prev, pltpu.roll(gm, 1, 0))
    dn = jnp.where(rows == tm - 1, g_next, pltpu.roll(gm, tm - 1, 0))
    cwb = cwb_ref[...]
    gc = up * cwb[0:1] + gm * cwb[1:2] + dn * cwb[2:3] + cwb[3:4]
    act = (gc * jax.nn.sigmoid(gc) * uv).astype(BF16)
    acc = jnp.where(j == 0, x_ref[...], o_ref[...])
    o_ref[...] = acc + jnp.dot(act, wo_ref[...].astype(BF16), preferred_element_type=F32)


def _ffn_layer(x, gain, w_in, cwb, w_out, layer, seq, tm, tf):
    m, d = x.shape
    f = w_out.shape[1]
    nf = f // tf
    prev, nxt = _halo_specs(tm, m, d, 2)
    return pl.pallas_call(
        functools.partial(_ffn_body, tm=tm, seq=seq),
        grid=(m // tm, nf),
        in_specs=[
            prev,
            pl.BlockSpec((tm, d), lambda i, j: (i, 0)),
            nxt,
            pl.BlockSpec((1, d), lambda i, j: (0, 0)),
            pl.BlockSpec((None, d, tf), lambda i, j: (layer, 0, j)),
            pl.BlockSpec((None, d, tf), lambda i, j: (layer, 0, j + nf)),
            pl.BlockSpec((None, 8, tf), lambda i, j: (layer, 0, j)),
            pl.BlockSpec((None, tf, d), lambda i, j: (layer, j, 0)),
        ],
        out_specs=pl.BlockSpec((tm, d), lambda i, j: (i, 0)),
        out_shape=jax.ShapeDtypeStruct((m, d), F32),
        scratch_shapes=[pltpu.VMEM((tm + 16, d), BF16)],
        compiler_params=_params("arbitrary", "arbitrary", vmem=V7X_VMEM_BYTES),
        name="ffn_convglu",
    )(x, x, x, gain, w_in, w_in, cwb, w_out)


def _pool_bands():
    t = np.arange(POOL_SUB)[:, None]
    s = np.arange(POOL_BAND_K)[None, :]
    bands = [((s >= t + POOL_HALO - w // 2) & (s <= t + POOL_HALO + (w - 1 - w // 2))) for w in POOL_WINDOWS]
    return np.stack(bands).astype(np.float32)


def _pool_body(xp_ref, x_ref, xn_ref, g_ref, band_ref, w_ref, sc_ref, o_ref, hb_ref, *, ts, seq):
    i = pl.program_id(0)
    data_rows = ts + 2 * POOL_HALO

    @pl.when(i == 0)
    def _():
        hb_ref[data_rows:, :] = jnp.zeros((hb_ref.shape[0] - data_rows, hb_ref.shape[1]), BF16)

    g = g_ref[...]
    x = x_ref[...]
    s0 = (i * ts) % seq
    h = _rms(x, g)

    hb_ref[0:POOL_HALO, :] = (_rms(xp_ref[...], g) * jnp.where(s0 != 0, 1.0, 0.0)).astype(BF16)
    hb_ref[POOL_HALO:POOL_HALO + ts, :] = h.astype(BF16)
    hb_ref[POOL_HALO + ts:data_rows, :] = (_rms(xn_ref[...], g) * jnp.where(s0 + ts != seq, 1.0, 0.0)).astype(BF16)

    pos = s0 + lax.broadcasted_iota(jnp.int32, (ts, 1), 0)
    dg = x.shape[1] // len(POOL_WINDOWS)
    for gi, w in enumerate(POOL_WINDOWS):
        left = w // 2
        right = w - 1 - left
        cols = slice(gi * dg, (gi + 1) * dg)
        band = band_ref[gi]
        wsum = []
        for r0 in range(0, ts, POOL_SUB):
            rows = slice(r0, r0 + POOL_BAND_K)
            wsum.append(jnp.dot(band, hb_ref[rows, cols], preferred_element_type=F32))
        wsum = jnp.concatenate(wsum, axis=0)
        cnt = (jnp.minimum(pos + right + 1, seq) - jnp.maximum(pos - left, 0)).astype(F32)
        pooled = (wsum * (1.0 / cnt) - h[:, cols]).astype(BF16)
        y = jnp.dot(pooled, w_ref[gi], preferred_element_type=F32) * sc_ref[:, cols]
        o_ref[:, cols] = x[:, cols] + y


def _pool_layer(x, gain, bands, pool_w, scale, layer, seq, ts):
    m, d = x.shape
    ng, dg = pool_w.shape[1], pool_w.shape[2]
    prev, nxt = _halo_specs(ts, m, d, 1, POOL_HALO)
    scratch = pltpu.VMEM((ts + POOL_BAND_K - POOL_SUB, d), BF16)
    return pl.pallas_call(
        functools.partial(_pool_body, ts=ts, seq=seq),
        grid=(m // ts,),
        in_specs=[
            prev,
            pl.BlockSpec((ts, d), lambda i: (i, 0)),
            nxt,
            pl.BlockSpec((1, d), lambda i: (0, 0)),
            pl.BlockSpec(bands.shape, lambda i: (0, 0, 0)),
            pl.BlockSpec((None, ng, dg, dg), lambda i: (layer, 0, 0, 0)),
            pl.BlockSpec((1, d), lambda i: (0, 0)),
        ],
        out_specs=pl.BlockSpec((ts, d), lambda i: (i, 0)),
        out_shape=jax.ShapeDtypeStruct((m, d), F32),
        scratch_shapes=[scratch],
        compiler_params=_params("arbitrary"),
        name="pool_mixer",
    )(x, x, x, gain, bands, pool_w, scale)


def _proj_res_body(a_ref, w_ref, x_ref, o_ref):
    o_ref[...] = x_ref[...] + jnp.dot(a_ref[...], w_ref[...], preferred_element_type=F32)


def _proj_residual(a, w, x, tm):
    m, d = x.shape
    k = a.shape[1]
    return pl.pallas_call(
        _proj_res_body,
        grid=(m // tm,),
        in_specs=[
            pl.BlockSpec((tm, k), lambda i: (i, 0)),
            pl.BlockSpec((k, d), lambda i: (0, 0)),
            pl.BlockSpec((tm, d), lambda i: (i, 0)),
        ],
        out_specs=pl.BlockSpec((tm, d), lambda i: (i, 0)),
        out_shape=jax.ShapeDtypeStruct((m, d), F32),
        compiler_params=_params("arbitrary"),
        name="proj_residual",
    )(a, w, x)


def _group_sumsq(z, e):
    return jnp.dot((z * z).astype(BF16), e, preferred_element_type=F32)


def _swa_qkv_body(x_ref, g_ref, w_ref, qg_ref, kg_ref, e_ref, q_ref, k_ref, v_ref):
    hn = _rms(x_ref[...], g_ref[...]).astype(BF16)
    qkv = jnp.dot(hn, w_ref[...], preferred_element_type=F32)
    e = e_ref[...]
    nq = q_ref.shape[1]
    nkv = k_ref.shape[1]
    cw = e.shape[0]
    inv_hd = 1.0 / SWA_HEAD_DIM

    def head_norm(z, gain):
        return z * lax.rsqrt(_group_sumsq(z, e) * inv_hd + EPS) * gain

    qg = qg_ref[...] * (SWA_HEAD_DIM ** -0.5 * LOG2E)
    for c in range(nq // cw):
        cols = slice(c * cw, (c + 1) * cw)
        q_ref[:, cols] = head_norm(qkv[:, cols], qg).astype(BF16)
    k_ref[...] = head_norm(qkv[:, nq:nq + nkv], kg_ref[...]).astype(BF16)
    v_ref[...] = qkv[:, nq + nkv:].astype(BF16)


def _swa_qkv(x, gain, w_qkv, qg, kg, e, tm):
    m, d = x.shape
    nkv = SWA_KV_HEADS * SWA_HEAD_DIM
    nq = w_qkv.shape[1] - 2 * nkv
    return pl.pallas_call(
        _swa_qkv_body,
        grid=(m // tm,),
        in_specs=[
            pl.BlockSpec((tm, d), lambda i: (i, 0)),
            pl.BlockSpec((1, d), lambda i: (0, 0)),
            pl.BlockSpec(w_qkv.shape, lambda i: (0, 0)),
            pl.BlockSpec((1, nkv), lambda i: (0, 0)),
            pl.BlockSpec((1, nkv), lambda i: (0, 0)),
            pl.BlockSpec(e.shape, lambda i: (0, 0)),
        ],
        out_specs=[
            pl.BlockSpec((tm, nq), lambda i: (i, 0)),
            pl.BlockSpec((tm, nkv), lambda i: (i, 0)),
            pl.BlockSpec((tm, nkv), lambda i: (i, 0)),
        ],
        out_shape=[
            jax.ShapeDtypeStruct((m, nq), BF16),
            jax.ShapeDtypeStruct((m, nkv), BF16),
            jax.ShapeDtypeStruct((m, nkv), BF16),
        ],
        compiler_params=_params("arbitrary"),
        name="swa_qkv",
    )(x, gain, w_qkv, qg, kg, e)


def _swa_slopes():
    return [float(2.0 ** (-8.0 * (i + 1) / SWA_HEADS)) for i in range(SWA_HEADS)]


def _swa_attn_body(sink_ref, q_ref, k_ref, v_ref, pc_ref, pr_ref, o_ref, *, seq):
    j = pl.program_id(1)
    hd = SWA_HEAD_DIM
    start = pl.multiple_of(jnp.clip(j * SWA_BLOCK - SWA_WINDOW, 0, seq - SWA_SPAN), SWA_BLOCK)
    kblk = k_ref[0, pl.ds(start, SWA_SPAN), :]
    vblk = v_ref[0, pl.ds(start, SWA_SPAN), :]
    pk = pr_ref[:, pl.ds(start, SWA_SPAN)]
    pq = pc_ref[...]
    qi = j * SWA_BLOCK + lax.broadcasted_iota(jnp.int32, (SWA_BLOCK, 1), 0)
    ki = start + lax.broadcasted_iota(jnp.int32, (1, SWA_SPAN), 1)
    in_win = jnp.abs(qi - ki) <= SWA_WINDOW
    dist = jnp.where(in_win, jnp.abs(pq - pk).astype(F32), jnp.inf)
    lane = lax.broadcasted_iota(jnp.int32, (1, 2 * hd), 1)
    slopes = _swa_slopes()
    zero = jnp.zeros((), BF16)
    lane_full = lax.broadcasted_iota(jnp.int32, (SWA_SPAN, 2 * hd), 1)
    ind_a = jnp.where(lane_full < hd, 1.0, 0.0).astype(BF16)
    ind_b = jnp.where(lane_full >= hd, 1.0, 0.0).astype(BF16)
    def pair_cols(mp):
        return [slice((mp * SWA_GROUP + g) * 2 * hd, (mp * SWA_GROUP + g + 1) * 2 * hd) for g in range(SWA_GROUP)]

    def score(mp):
        kp = kblk[:, mp * 2 * hd:(mp + 1) * 2 * hd]
        k2 = jnp.concatenate([jnp.where(lane < hd, kp, zero), jnp.where(lane >= hd, kp, zero)], axis=0)
        q_all = jnp.concatenate([q_ref[0, :, cols] for cols in pair_cols(mp)], axis=0)
        return lax.dot_general(q_all, k2, (((1,), (1,)), ((), ())), preferred_element_type=F32)

    def softmax(mp, s_all):
        ps, sink_terms = [], []
        for g in range(SWA_GROUP):
            ha = (2 * mp) * SWA_GROUP + g
            hb = (2 * mp + 1) * SWA_GROUP + g
            s = s_all[g * SWA_BLOCK:(g + 1) * SWA_BLOCK]
            sa = s[:, :SWA_SPAN] - (LOG2E * slopes[ha]) * dist
            sb = s[:, SWA_SPAN:] - (LOG2E * slopes[hb]) * dist
            sink_a = sink_ref[ha] * LOG2E
            sink_b = sink_ref[hb] * LOG2E
            ma = jnp.maximum(jnp.max(sa, axis=-1, keepdims=True), sink_a)
            mb = jnp.maximum(jnp.max(sb, axis=-1, keepdims=True), sink_b)
            ps.append(jnp.concatenate([jnp.exp2(sa - ma), jnp.exp2(sb - mb)], axis=1).astype(BF16))
            sink_terms.append(jnp.where(lane < hd, jnp.exp2(sink_a - ma), jnp.exp2(sink_b - mb)))
        return jnp.concatenate(ps, axis=0), sink_terms

    def weigh(mp, p_all, sink_terms):
        vp = vblk[:, mp * 2 * hd:(mp + 1) * 2 * hd]
        v2 = jnp.concatenate([jnp.concatenate([jnp.where(lane < hd, vp, zero), ind_a], axis=1),
                              jnp.concatenate([jnp.where(lane >= hd, vp, zero), ind_b], axis=1)], axis=0)
        o_all = jnp.dot(p_all, v2, preferred_element_type=F32)
        for g, cols in enumerate(pair_cols(mp)):
            og = o_all[g * SWA_BLOCK:(g + 1) * SWA_BLOCK]
            o_ref[0, :, cols] = (og[:, :2 * hd] / (og[:, 2 * hd:] + sink_terms[g])).astype(BF16)

    for mp in range(SWA_KV_HEADS // 2):
        weigh(mp, *softmax(mp, score(mp)))


def _swa_attention(q, k, v, pos_col, pos_row, sinks, seq):
    b = q.shape[0]
    nq = q.shape[2]
    nkv = k.shape[2]
    return pl.pallas_call(
        functools.partial(_swa_attn_body, seq=seq),
        grid=(b, seq // SWA_BLOCK),
        in_specs=[
            pl.BlockSpec(memory_space=pltpu.SMEM),
            pl.BlockSpec((1, SWA_BLOCK, nq), lambda bi, j: (bi, j, 0)),
            pl.BlockSpec((1, seq, nkv), lambda bi, j: (bi, 0, 0)),
            pl.BlockSpec((1, seq, nkv), lambda bi, j: (bi, 0, 0)),
            pl.BlockSpec((SWA_BLOCK, 1), lambda bi, j: (j, 0)),
            pl.BlockSpec((1, seq), lambda bi, j: (0, 0)),
        ],
        out_specs=pl.BlockSpec((1, SWA_BLOCK, nq), lambda bi, j: (bi, j, 0)),
        out_shape=jax.ShapeDtypeStruct(q.shape, BF16),
        compiler_params=_params("arbitrary", "arbitrary"),
        name="swa_attention",
    )(sinks, q, k, v, pos_col, pos_row)


def _rope64(z, c, s1, s2):
    return z * c + pltpu.roll(z, V7X_LANES - MLA_ROPE // 2, 1) * s1 + pltpu.roll(z, MLA_ROPE // 2, 1) * s2


def _slot_rms(z, gain):
    ms = jnp.sum(z * z, axis=-1, keepdims=True) * (1.0 / MLA_ROPE)
    return z * lax.rsqrt(ms + EPS) * gain


def _mla_down_body(x_ref, g_ref, w_ref, qa_ref, kva_ref, kr_ref, c_ref, s1_ref, s2_ref,
                   cq_ref, ckv_ref, kpe_ref):
    hn = _rms(x_ref[...], g_ref[...]).astype(BF16)
    d = jnp.dot(hn, w_ref[...], preferred_element_type=F32)
    cq_ref[...] = _rms(d[:, :MLA_Q_RANK], qa_ref[...]).astype(BF16)
    ckv_ref[...] = _rms(d[:, MLA_Q_RANK:MLA_Q_RANK + MLA_KV_RANK], kva_ref[...]).astype(BF16)
    kp = _slot_rms(d[:, MLA_Q_RANK + MLA_KV_RANK:], kr_ref[...])
    kpe_ref[...] = _rope64(kp, c_ref[...], s1_ref[...], s2_ref[...]).astype(BF16)


def _mla_down(x, gain, w_down, qa, kva, kr, c, s1, s2, seq, tm):
    m, d = x.shape
    nt = seq // tm
    tab = pl.BlockSpec((tm, V7X_LANES), lambda i: (i % nt, 0))
    row = lambda n: pl.BlockSpec((1, n), lambda i: (0, 0))
    return pl.pallas_call(
        _mla_down_body,
        grid=(m // tm,),
        in_specs=[
            pl.BlockSpec((tm, d), lambda i: (i, 0)),
            row(d),
            pl.BlockSpec(w_down.shape, lambda i: (0, 0)),
            row(MLA_Q_RANK), row(MLA_KV_RANK), row(V7X_LANES),
            tab, tab, tab,
        ],
        out_specs=[
            pl.BlockSpec((tm, MLA_Q_RANK), lambda i: (i, 0)),
            pl.BlockSpec((tm, MLA_KV_RANK), lambda i: (i, 0)),
            pl.BlockSpec((tm, V7X_LANES), lambda i: (i, 0)),
        ],
        out_shape=[
            jax.ShapeDtypeStruct((m, MLA_Q_RANK), BF16),
            jax.ShapeDtypeStruct((m, MLA_KV_RANK), BF16),
            jax.ShapeDtypeStruct((m, V7X_LANES), BF16),
        ],
        compiler_params=_params("arbitrary"),
        name="mla_down",
    )(x, gain, w_down, qa, kva, kr, c, s1, s2)


def _mla_up_body(cq_ref, ckv_ref, kpe_ref, wuqt_ref, wuk_ref, wuvt_ref, qn_ref, qr_ref, kn_ref,
                 cos_ref, sin_ref, qt_ref, k_ref, vt_ref):
    ckv = ckv_ref[...]
    nt = (((1,), (1,)), ((), ()))
    qt = lax.dot_general(wuqt_ref[...], cq_ref[...], nt, preferred_element_type=F32)
    vt = lax.dot_general(wuvt_ref[...], ckv, nt, preferred_element_type=F32)
    kn = jnp.dot(ckv, wuk_ref[...], preferred_element_type=F32)
    kpe = kpe_ref[...]
    qng = qn_ref[...]
    qrg = qr_ref[...]
    kng = kn_ref[...]
    cos, sin = cos_ref[...], sin_ref[...]
    tm = ckv.shape[0]
    half = MLA_ROPE // 2
    ones = jnp.ones((MLA_VT_ROWS - MLA_V, tm), BF16)
    zeros = jnp.zeros((MLA_SLOT - MLA_NOPE - MLA_ROPE, tm), BF16)
    for h in range(MLA_HEADS):
        lo = h * MLA_SLOT
        mid = lo + MLA_NOPE
        qn = qt[lo:mid]
        ms = jnp.mean(qn * qn, axis=0, keepdims=True)
        qt_ref[0, h, 0:MLA_NOPE, :] = (qn * lax.rsqrt(ms + EPS) * qng).astype(BF16)
        x1 = qt[mid:mid + half]
        x2 = qt[mid + half:mid + 2 * half]
        ms = (jnp.sum(x1 * x1, axis=0, keepdims=True)
              + jnp.sum(x2 * x2, axis=0, keepdims=True)) * (1.0 / MLA_ROPE)
        r = lax.rsqrt(ms + EPS)
        x1 = x1 * r * qrg[0:half]
        x2 = x2 * r * qrg[half:2 * half]
        qt_ref[0, h, MLA_NOPE:MLA_NOPE + half, :] = (x1 * cos - x2 * sin).astype(BF16)
        qt_ref[0, h, MLA_NOPE + half:MLA_NOPE + 2 * half, :] = (x2 * cos + x1 * sin).astype(BF16)
        qt_ref[0, h, MLA_NOPE + 2 * half:MLA_SLOT, :] = zeros
        k_ref[0, h, :, 0:MLA_NOPE] = _rms(kn[:, h * MLA_NOPE:(h + 1) * MLA_NOPE], kng).astype(BF16)
        k_ref[0, h, :, MLA_NOPE:MLA_SLOT] = kpe
        vt_ref[0, h, 0:MLA_V, :] = vt[h * MLA_V:(h + 1) * MLA_V, :].astype(BF16)
        vt_ref[0, h, MLA_V:MLA_VT_ROWS, :] = ones


def _mla_up(cq, ckv, kpe, w_uqt, w_uk, w_uvt, qn, qr, kn, cos_t, sin_t, b, seq, tm):
    nt = seq // tm
    tok = lambda n: pl.BlockSpec((tm, n), lambda bi, i: (bi * nt + i, 0))
    full = lambda a: pl.BlockSpec(a.shape, lambda bi, i: (0, 0))
    tab = pl.BlockSpec((MLA_ROPE // 2, tm), lambda bi, i: (0, i))
    out = pl.BlockSpec((1, MLA_HEADS, tm, MLA_SLOT), lambda bi, i: (bi, 0, i, 0))
    shape = jax.ShapeDtypeStruct((b, MLA_HEADS, seq, MLA_SLOT), BF16)
    out_q = pl.BlockSpec((1, MLA_HEADS, MLA_SLOT, tm), lambda bi, i: (bi, 0, 0, i))
    shape_q = jax.ShapeDtypeStruct((b, MLA_HEADS, MLA_SLOT, seq), BF16)
    out_v = pl.BlockSpec((1, MLA_HEADS, MLA_VT_ROWS, tm), lambda bi, i: (bi, 0, 0, i))
    shape_v = jax.ShapeDtypeStruct((b, MLA_HEADS, MLA_VT_ROWS, seq), BF16)
    return pl.pallas_call(
        _mla_up_body,
        grid=(b, nt),
        in_specs=[tok(MLA_Q_RANK), tok(MLA_KV_RANK), tok(V7X_LANES), full(w_uqt), full(w_uk), full(w_uvt),
                  full(qn), full(qr), full(kn), tab, tab],
        out_specs=[out_q, out, out_v],
        out_shape=[shape_q, shape, shape_v],
        compiler_params=_params("arbitrary", "arbitrary"),
        name="mla_up",
    )(cq, ckv, kpe, w_uqt, w_uk, w_uvt, qn, qr, kn, cos_t, sin_t)


def _mla_attn_body(q_ref, k_ref, vt_ref, o_ref, *scratch):
    step = pl.program_id(0)
    n_sub = len(scratch) // 4
    s_bufs, m_bufs = scratch[:2 * n_sub], scratch[2 * n_sub:]
    sub = s_bufs[0].shape[1]

    @pl.when(step == 0)
    def _():
        for r in s_bufs[n_sub:] + m_bufs[n_sub:]:
            r[...] = jnp.zeros(r.shape, F32)

    def work(t, s_new, m_new, s_old, m_old):
        cols = slice(t * sub, (t + 1) * sub)
        q = q_ref[0, 0, :, cols]
        m_prev = m_old[...]
        nc = s_new.shape[0] // MLA_KEY_CHUNK
        chunk = lambda c: slice(c * MLA_KEY_CHUNK, (c + 1) * MLA_KEY_CHUNK)

        def probs(c):
            return jnp.exp2(s_old[chunk(c), :] - m_prev).astype(BF16)

        def weigh(c, p, ot):
            part = jnp.dot(vt_ref[0, 0, :, chunk(c)], p, preferred_element_type=F32)
            return part if ot is None else ot + part

        def score(c, m_run):
            st = jnp.dot(k_ref[0, 0, chunk(c), :], q, preferred_element_type=F32)
            s_new[chunk(c), :] = st
            mc = jnp.max(st, axis=0, keepdims=True)
            return mc if m_run is None else jnp.maximum(m_run, mc)

        ot = weigh(0, probs(0), None)
        m_run = None
        for c in range(nc - 1):
            m_run = score(c, m_run)
            ot = weigh(c + 1, probs(c + 1), ot)
        o = ot[0:MLA_V] / ot[MLA_V:MLA_V + 1]
        o_ref[0, cols, :] = o.T.astype(BF16)
        m_new[...] = score(nc - 1, m_run)

    def run(new, old):
        for t in range(n_sub):
            work(t, s_bufs[new * n_sub + t], m_bufs[new * n_sub + t], s_bufs[old * n_sub + t], m_bufs[old * n_sub + t])

    @pl.when(step % 2 == 0)
    def _():
        run(0, 1)

    @pl.when(step % 2 == 1)
    def _():
        run(1, 0)


def _mla_attention(qt, k, vt, tq):
    b, nh, seq, slot = k.shape
    nq = seq // tq
    n = b * nh * nq
    n_sub = tq // MLA_SUB_TILE

    def tile(t):
        return t // (nh * nq), (t // nq) % nh, t % nq

    def q_map(s):
        bi, h, i = tile(jnp.minimum(s, n - 1))
        return bi, h, 0, i

    def k_map(s):
        bi, h, _ = tile(jnp.minimum(s, n - 1))
        return bi, h, 0, 0

    def vt_map(s):
        bi, h, _ = tile(jnp.maximum(s - 1, 0))
        return bi, h, 0, 0

    def o_map(s):
        bi, h, i = tile(jnp.maximum(s - 1, 0))
        return bi, i, h

    return pl.pallas_call(
        _mla_attn_body,
        grid=(n + 1,),
        in_specs=[
            pl.BlockSpec((1, 1, slot, tq), q_map),
            pl.BlockSpec((1, 1, seq, slot), k_map),
            pl.BlockSpec((1, 1, MLA_VT_ROWS, seq), vt_map),
        ],
        out_specs=pl.BlockSpec((1, tq, MLA_V), o_map),
        out_shape=jax.ShapeDtypeStruct((b, seq, nh * MLA_V), BF16),
        scratch_shapes=([pltpu.VMEM((seq, MLA_SUB_TILE), F32)] * (2 * n_sub)
                        + [pltpu.VMEM((1, MLA_SUB_TILE), F32)] * (2 * n_sub)),
        compiler_params=_params("arbitrary"),
        name="mla_attention",
    )(qt, k, vt)


def _swa_pair_perm(w_q_cols):
    lead = w_q_cols.shape[:-1]
    w = w_q_cols.reshape(lead + (2, 2, SWA_GROUP, SWA_HEAD_DIM))
    w = jnp.swapaxes(w, -3, -2)
    return w.reshape(lead + (SWA_HEADS * SWA_HEAD_DIM,))


def _rope_tables(positions):
    half = MLA_ROPE // 2
    inv = ROPE_THETA ** (-jnp.arange(0, MLA_ROPE, 2, dtype=F32) / MLA_ROPE)
    ang = positions.astype(F32)[:, None] * inv[None, :]
    cos, sin = jnp.cos(ang), jnp.sin(ang)
    z = jnp.zeros_like(cos)
    pad = jnp.zeros((positions.shape[0], V7X_LANES - MLA_ROPE), F32)
    c = jnp.concatenate([cos, cos, pad], axis=1)
    s1 = jnp.concatenate([-sin, z, pad], axis=1)
    s2 = jnp.concatenate([z, sin, pad], axis=1)
    return c, s1, s2, cos.T, sin.T


def _pad_lanes(v, n):
    return jnp.pad(v, ((0, 0), (0, n - v.shape[1])))


def kernel(x, positions, norm_mix_g, norm_ffn_g, pool_w, pool_scale, swa_w_qkv, swa_q_gain, swa_k_gain, swa_sinks, swa_w_o, mla_w_down, mla_q_a_gain, mla_kv_a_gain, mla_w_uq, mla_w_ukv, mla_qn_gain, mla_qr_gain, mla_kn_gain, mla_kr_gain, mla_w_o, ffn_w_in, ffn_conv_w, ffn_conv_b, ffn_w_out):
    b, seq, d = x.shape
    m = b * seq
    depth = norm_mix_g.shape[0]
    t = _tiles(m, seq)
    xf = x.reshape(m, d)

    f = ffn_w_out.shape[1]
    cwb = jnp.concatenate([ffn_conv_w, ffn_conv_b[:, None, :], jnp.zeros((depth, 4, f), F32)], axis=1)
    pool_wb = pool_w.astype(BF16)
    pool_bands = jnp.asarray(_pool_bands(), dtype=BF16)

    nq = SWA_HEADS * SWA_HEAD_DIM
    pos_col = positions.reshape(seq, 1)
    pos_row = positions.reshape(1, seq)
    blk = np.kron(np.eye(4, dtype=np.float32), np.ones((SWA_HEAD_DIM, SWA_HEAD_DIM), np.float32))
    e_blk = jnp.asarray(blk, dtype=BF16)
    c_tab, s1_tab, s2_tab, cos_t, sin_t = _rope_tables(positions)

    for i in range(depth):
        kind = i % N_MIXERS
        j = i // N_MIXERS
        gain = norm_mix_g[i][None, :]
        if kind == 0:
            xf = _pool_layer(xf, gain, pool_bands, pool_wb, pool_scale[j][None, :], j, seq, t["pool_ts"])
        elif kind == 1:
            w = swa_w_qkv[j].astype(BF16)
            w_qkv = jnp.concatenate([_swa_pair_perm(w[:, :nq]), w[:, nq:]], axis=1)
            w_o = _swa_pair_perm(swa_w_o[j].astype(BF16).T).T
            qg = jnp.tile(swa_q_gain[j], 4)[None, :]
            kg = jnp.tile(swa_k_gain[j], 4)[None, :]
            q, k, v = _swa_qkv(xf, gain, w_qkv, qg, kg, e_blk, t["proj_tm"])
            o = _swa_attention(q.reshape(b, seq, -1), k.reshape(b, seq, -1), v.reshape(b, seq, -1),
                               pos_col, pos_row, swa_sinks[j], seq)
            xf = _proj_residual(o.reshape(m, -1), w_o, xf, t["res_tm"])
        else:
            n_lat = MLA_Q_RANK + MLA_KV_RANK
            w_down = _pad_lanes(mla_w_down[j].astype(BF16), n_lat + V7X_LANES)
            wq = mla_w_uq[j].astype(BF16).reshape(MLA_Q_RANK, MLA_HEADS, MLA_NOPE + MLA_ROPE)
            wq = jnp.pad(wq, ((0, 0), (0, 0), (0, MLA_SLOT - MLA_NOPE - MLA_ROPE)))
            w_uqt = wq.reshape(MLA_Q_RANK, MLA_HEADS * MLA_SLOT).T
            wkv = mla_w_ukv[j].astype(BF16).reshape(MLA_KV_RANK, MLA_HEADS, MLA_NOPE + MLA_V)
            w_uk = wkv[:, :, :MLA_NOPE].reshape(MLA_KV_RANK, MLA_HEADS * MLA_NOPE)
            w_uvt = wkv[:, :, MLA_NOPE:].reshape(MLA_KV_RANK, MLA_HEADS * MLA_V).T
            up_tm = t["up_tm"]
            q_scale = (MLA_NOPE + MLA_ROPE) ** -0.5 * LOG2E
            qn = jnp.broadcast_to((mla_qn_gain[j] * q_scale)[:, None], (MLA_NOPE, up_tm))
            qr = jnp.broadcast_to((mla_qr_gain[j] * q_scale)[:, None], (MLA_ROPE, up_tm))
            kr = _pad_lanes(mla_kr_gain[j][None, :], V7X_LANES)
            cq, ckv, kpe = _mla_down(xf, gain, w_down, mla_q_a_gain[j][None, :], mla_kv_a_gain[j][None, :], kr,
                                     c_tab, s1_tab, s2_tab, seq, t["proj_tm"])
            qt, k, vt = _mla_up(cq, ckv, kpe, w_uqt, w_uk, w_uvt, qn, qr, mla_kn_gain[j][None, :],
                                cos_t, sin_t, b, seq, up_tm)
            o = _mla_attention(qt, k, vt, t["attn_tq"])
            xf = _proj_residual(o.reshape(m, -1), mla_w_o[j].astype(BF16), xf, t["res_tm"])
        xf = _ffn_layer(xf, norm_ffn_g[i][None, :], ffn_w_in, cwb, ffn_w_out, i, seq, t["ffn_tm"], t["ffn_tf"])
    return xf.reshape(b, seq, d)
```

```python
import functools

import numpy as np
import jax
import jax.numpy as jnp
from jax import lax
from jax.experimental import pallas as pl
from jax.experimental.pallas import tpu as pltpu

F32 = jnp.float32
BF16 = jnp.bfloat16

EPS = 1e-6
LOG2E = 1.4426950408889634
N_MIXERS = 3
POOL_WINDOWS = (2, 4, 8, 16)
POOL_HALO = 16
POOL_SUB = 128
POOL_BAND_K = 256
SWA_HEADS = 32
SWA_KV_HEADS = 4
SWA_GROUP = SWA_HEADS // SWA_KV_HEADS
SWA_HEAD_DIM = 64
SWA_WINDOW = 128
SWA_BLOCK = 128
SWA_SPAN = SWA_BLOCK + 2 * SWA_WINDOW
MLA_HEADS = 16
MLA_NOPE = 128
MLA_ROPE = 64
MLA_V = 128
MLA_Q_RANK = 512
MLA_KV_RANK = 512
MLA_SLOT = 256
MLA_VT_ROWS = MLA_V + 16
MLA_KEY_CHUNK = 512
MLA_SUB_TILE = 512
ROPE_THETA = 10000.0

V7X_LANES = 128
V7X_SUBLANES = 8
V7X_VMEM_BYTES = 64 * 1024 * 1024
V7X_VMEM_LIMIT_BYTES = 56 * 1024 * 1024


def _tiles(m, seq):
    def fit(t):
        while seq % t:
            t //= 2
        return t
    return dict(ffn_tm=fit(1024), ffn_tf=512, pool_ts=fit(512), proj_tm=fit(512), up_tm=fit(256),
                attn_tq=fit(1024), res_tm=fit(512))


def _params(*sem, vmem=V7X_VMEM_LIMIT_BYTES):
    return pltpu.CompilerParams(dimension_semantics=sem, vmem_limit_bytes=vmem)


def _rms(x, g):
    ms = jnp.mean(x * x, axis=-1, keepdims=True)
    return x * lax.rsqrt(ms + EPS) * g


def _halo_specs(tm, m, d, nargs, rows=V7X_SUBLANES):
    r = tm // rows
    last = m // rows - 1
    if nargs == 1:
        prev = pl.BlockSpec((rows, d), lambda i: (jnp.maximum(i * r - 1, 0), 0))
        nxt = pl.BlockSpec((rows, d), lambda i: (jnp.minimum((i + 1) * r, last), 0))
    else:
        prev = pl.BlockSpec((rows, d), lambda i, j: (jnp.maximum(i * r - 1, 0), 0))
        nxt = pl.BlockSpec((rows, d), lambda i, j: (jnp.minimum((i + 1) * r, last), 0))
    return prev, nxt


def _ffn_norm_tile(xp_ref, x_ref, xn_ref, g_ref, hn_ref, first_row, *, tm, seq):
    g = g_ref[...]
    hn_ref[0:tm, :] = _rms(x_ref[...], g).astype(BF16)
    s0 = first_row % seq
    keep_prev = jnp.where(s0 != 0, 7, -1)
    keep_next = jnp.where(s0 + tm != seq, 8, -1)
    ext = jnp.concatenate([_rms(xp_ref[...], g), _rms(xn_ref[...], g)], axis=0)
    r = lax.broadcasted_iota(jnp.int32, (16, 1), 0)
    ext = jnp.where((r == keep_prev) | (r == keep_next), ext, 0.0)
    hn_ref[tm:tm + 16, :] = ext.astype(BF16)


def _ffn_body(xp_ref, x_ref, xn_ref, g_ref, wg_ref, wv_ref, cwb_ref, wo_ref, o_ref, hn_ref, *, tm, seq):
    i = pl.program_id(0)
    j = pl.program_id(1)

    @pl.when(j == 0)
    def _():
        _ffn_norm_tile(xp_ref, x_ref, xn_ref, g_ref, hn_ref, i * tm, tm=tm, seq=seq)

    ug = jnp.dot(hn_ref[...], wg_ref[...].astype(BF16), preferred_element_type=F32)
    uv = jnp.dot(hn_ref[0:tm, :], wv_ref[...].astype(BF16), preferred_element_type=F32)
    gm = ug[0:tm]
    g_prev = ug[tm + 7:tm + 8]
    g_next = ug[tm + 8:tm + 9]
    rows = lax.broadcasted_iota(jnp.int32, (tm, 1), 0)
    up = jnp.where(rows == 0, g_prev, pltpu.roll(gm, 1, 0))
    dn = jnp.where(rows == tm - 1, g_next, pltpu.roll(gm, tm - 1, 0))
    cwb = cwb_ref[...]
    gc = up * cwb[0:1] + gm * cwb[1:2] + dn * cwb[2:3] + cwb[3:4]
    act = (gc * jax.nn.sigmoid(gc) * uv).astype(BF16)
    acc = jnp.where(j == 0, x_ref[...], o_ref[...])
    o_ref[...] = acc + jnp.dot(act, wo_ref[...].astype(BF16), preferred_element_type=F32)


def _ffn_layer(x, gain, w_in, cwb, w_out, layer, seq, tm, tf):
    m, d = x.shape
    f = w_out.shape[1]
    nf = f // tf
    prev, nxt = _halo_specs(tm, m, d, 2)
    return pl.pallas_call(
        functools.partial(_ffn_body, tm=tm, seq=seq),
        grid=(m // tm, nf),
        in_specs=[
            prev,
            pl.BlockSpec((tm, d), lambda i, j: (i, 0)),
            nxt,
            pl.BlockSpec((1, d), lambda i, j: (0, 0)),
            pl.BlockSpec((None, d, tf), lambda i, j: (layer, 0, j)),
            pl.BlockSpec((None, d, tf), lambda i, j: (layer, 0, j + nf)),
            pl.BlockSpec((None, 8, tf), lambda i, j: (layer, 0, j)),
            pl.BlockSpec((None, tf, d), lambda i, j: (layer, j, 0)),
        ],
        out_specs=pl.BlockSpec((tm, d), lambda i, j: (i, 0)),
        out_shape=jax.ShapeDtypeStruct((m, d), F32),
        scratch_shapes=[pltpu.VMEM((tm + 16, d), BF16)],
        compiler_params=_params("arbitrary", "arbitrary", vmem=V7X_VMEM_BYTES),
        name="ffn_convglu",
    )(x, x, x, gain, w_in, w_in, cwb, w_out)


def _pool_bands():
    t = np.arange(POOL_SUB)[:, None]
    s = np.arange(POOL_BAND_K)[None, :]
    bands = [((s >= t + POOL_HALO - w // 2) & (s <= t + POOL_HALO + (w - 1 - w // 2))) for w in POOL_WINDOWS]
    return np.stack(bands).astype(np.float32)


def _pool_body(xp_ref, x_ref, xn_ref, g_ref, band_ref, w_ref, sc_ref, o_ref, hb_ref, *, ts, seq):
    i = pl.program_id(0)
    data_rows = ts + 2 * POOL_HALO

    @pl.when(i == 0)
    def _():
        hb_ref[data_rows:, :] = jnp.zeros((hb_ref.shape[0] - data_rows, hb_ref.shape[1]), BF16)

    g = g_ref[...]
    x = x_ref[...]
    s0 = (i * ts) % seq
    h = _rms(x, g)

    hb_ref[0:POOL_HALO, :] = (_rms(xp_ref[...], g) * jnp.where(s0 != 0, 1.0, 0.0)).astype(BF16)
    hb_ref[POOL_HALO:POOL_HALO + ts, :] = h.astype(BF16)
    hb_ref[POOL_HALO + ts:data_rows, :] = (_rms(xn_ref[...], g) * jnp.where(s0 + ts != seq, 1.0, 0.0)).astype(BF16)

    pos = s0 + lax.broadcasted_iota(jnp.int32, (ts, 1), 0)
    dg = x.shape[1] // len(POOL_WINDOWS)
    for gi, w in enumerate(POOL_WINDOWS):
        left = w // 2
        right = w - 1 - left
        cols = slice(gi * dg, (gi + 1) * dg)
        band = band_ref[gi]
        wsum = []
        for r0 in range(0, ts, POOL_SUB):
            rows = slice(r0, r0 + POOL_BAND_K)
            wsum.append(jnp.dot(band, hb_ref[rows, cols], preferred_element_type=F32))
        wsum = jnp.concatenate(wsum, axis=0)
        cnt = (jnp.minimum(pos + right + 1, seq) - jnp.maximum(pos - left, 0)).astype(F32)
        pooled = (wsum * (1.0 / cnt) - h[:, cols]).astype(BF16)
        y = jnp.dot(pooled, w_ref[gi], preferred_element_type=F32) * sc_ref[:, cols]
        o_ref[:, cols] = x[:, cols] + y


def _pool_layer(x, gain, bands, pool_w, scale, layer, seq, ts):
    m, d = x.shape
    ng, dg = pool_w.shape[1], pool_w.shape[2]
    prev, nxt = _halo_specs(ts, m, d, 1, POOL_HALO)
    scratch = pltpu.VMEM((ts + POOL_BAND_K - POOL_SUB, d), BF16)
    return pl.pallas_call(
        functools.partial(_pool_body, ts=ts, seq=seq),
        grid=(m // ts,),
        in_specs=[
            prev,
            pl.BlockSpec((ts, d), lambda i: (i, 0)),
            nxt,
            pl.BlockSpec((1, d), lambda i: (0, 0)),
            pl.BlockSpec(bands.shape, lambda i: (0, 0, 0)),
            pl.BlockSpec((None, ng, dg, dg), lambda i: (layer, 0, 0, 0)),
            pl.BlockSpec((1, d), lambda i: (0, 0)),
        ],
        out_specs=pl.BlockSpec((ts, d), lambda i: (i, 0)),
        out_shape=jax.ShapeDtypeStruct((m, d), F32),
        scratch_shapes=[scratch],
        compiler_params=_params("arbitrary"),
        name="pool_mixer",
    )(x, x, x, gain, bands, pool_w, scale)


def _proj_res_body(a_ref, w_ref, x_ref, o_ref):
    o_ref[...] = x_ref[...] + jnp.dot(a_ref[...], w_ref[...], preferred_element_type=F32)


def _proj_residual(a, w, x, tm):
    m, d = x.shape
    k = a.shape[1]
    return pl.pallas_call(
        _proj_res_body,
        grid=(m // tm,),
        in_specs=[
            pl.BlockSpec((tm, k), lambda i: (i, 0)),
            pl.BlockSpec((k, d), lambda i: (0, 0)),
            pl.BlockSpec((tm, d), lambda i: (i, 0)),
        ],
        out_specs=pl.BlockSpec((tm, d), lambda i: (i, 0)),
        out_shape=jax.ShapeDtypeStruct((m, d), F32),
        compiler_params=_params("arbitrary"),
        name="proj_residual",
    )(a, w, x)


def _group_sumsq(z, e):
    return jnp.dot((z * z).astype(BF16), e, preferred_element_type=F32)


def _swa_qkv_body(x_ref, g_ref, w_ref, qg_ref, kg_ref, e_ref, q_ref, k_ref, v_ref):
    hn = _rms(x_ref[...], g_ref[...]).astype(BF16)
    qkv = jnp.dot(hn, w_ref[...], preferred_element_type=F32)
    e = e_ref[...]
    nq = q_ref.shape[1]
    nkv = k_ref.shape[1]
    cw = e.shape[0]
    inv_hd = 1.0 / SWA_HEAD_DIM

    def head_norm(z, gain):
        return z * lax.rsqrt(_group_sumsq(z, e) * inv_hd + EPS) * gain

    qg = qg_ref[...] * (SWA_HEAD_DIM ** -0.5 * LOG2E)
    for c in range(nq // cw):
        cols = slice(c * cw, (c + 1) * cw)
        q_ref[:, cols] = head_norm(qkv[:, cols], qg).astype(BF16)
    k_ref[...] = head_norm(qkv[:, nq:nq + nkv], kg_ref[...]).astype(BF16)
    v_ref[...] = qkv[:, nq + nkv:].astype(BF16)


def _swa_qkv(x, gain, w_qkv, qg, kg, e, tm):
    m, d = x.shape
    nkv = SWA_KV_HEADS * SWA_HEAD_DIM
    nq = w_qkv.shape[1] - 2 * nkv
    return pl.pallas_call(
        _swa_qkv_body,
        grid=(m // tm,),
        in_specs=[
            pl.BlockSpec((tm, d), lambda i: (i, 0)),
            pl.BlockSpec((1, d), lambda i: (0, 0)),
            pl.BlockSpec(w_qkv.shape, lambda i: (0, 0)),
            pl.BlockSpec((1, nkv), lambda i: (0, 0)),
            pl.BlockSpec((1, nkv), lambda i: (0, 0)),
            pl.BlockSpec(e.shape, lambda i: (0, 0)),
        ],
        out_specs=[
            pl.BlockSpec((tm, nq), lambda i: (i, 0)),
            pl.BlockSpec((tm, nkv), lambda i: (i, 0)),
            pl.BlockSpec((tm, nkv), lambda i: (i, 0)),
        ],
        out_shape=[
            jax.ShapeDtypeStruct((m, nq), BF16),
            jax.ShapeDtypeStruct((m, nkv), BF16),
            jax.ShapeDtypeStruct((m, nkv), BF16),
        ],
        compiler_params=_params("arbitrary"),
        name="swa_qkv",
    )(x, gain, w_qkv, qg, kg, e)


def _swa_slopes():
    return [float(2.0 ** (-8.0 * (i + 1) / SWA_HEADS)) for i in range(SWA_HEADS)]


def _swa_attn_body(sink_ref, q_ref, k_ref, v_ref, pc_ref, pr_ref, o_ref, *, seq):
    j = pl.program_id(1)
    hd = SWA_HEAD_DIM
    start = pl.multiple_of(jnp.clip(j * SWA_BLOCK - SWA_WINDOW, 0, seq - SWA_SPAN), SWA_BLOCK)
    kblk = k_ref[0, pl.ds(start, SWA_SPAN), :]
    vblk = v_ref[0, pl.ds(start, SWA_SPAN), :]
    pk = pr_ref[:, pl.ds(start, SWA_SPAN)]
    pq = pc_ref[...]
    qi = j * SWA_BLOCK + lax.broadcasted_iota(jnp.int32, (SWA_BLOCK, 1), 0)
    ki = start + lax.broadcasted_iota(jnp.int32, (1, SWA_SPAN), 1)
    in_win = jnp.abs(qi - ki) <= SWA_WINDOW
    dist = jnp.where(in_win, jnp.abs(pq - pk).astype(F32), jnp.inf)
    lane = lax.broadcasted_iota(jnp.int32, (1, 2 * hd), 1)
    slopes = _swa_slopes()
    zero = jnp.zeros((), BF16)
    lane_full = lax.broadcasted_iota(jnp.int32, (SWA_SPAN, 2 * hd), 1)
    ind_a = jnp.where(lane_full < hd, 1.0, 0.0).astype(BF16)
    ind_b = jnp.where(lane_full >= hd, 1.0, 0.0).astype(BF16)
    def pair_cols(mp):
        return [slice((mp * SWA_GROUP + g) * 2 * hd, (mp * SWA_GROUP + g + 1) * 2 * hd) for g in range(SWA_GROUP)]

    def score(mp):
        kp = kblk[:, mp * 2 * hd:(mp + 1) * 2 * hd]
        k2 = jnp.concatenate([jnp.where(lane < hd, kp, zero), jnp.where(lane >= hd, kp, zero)], axis=0)
        q_all = jnp.concatenate([q_ref[0, :, cols] for cols in pair_cols(mp)], axis=0)
        return lax.dot_general(q_all, k2, (((1,), (1,)), ((), ())), preferred_element_type=F32)

    def softmax(mp, s_all):
        ps, sink_terms = [], []
        for g in range(SWA_GROUP):
            ha = (2 * mp) * SWA_GROUP + g
            hb = (2 * mp + 1) * SWA_GROUP + g
            s = s_all[g * SWA_BLOCK:(g + 1) * SWA_BLOCK]
            sa = s[:, :SWA_SPAN] - (LOG2E * slopes[ha]) * dist
            sb = s[:, SWA_SPAN:] - (LOG2E * slopes[hb]) * dist
            sink_a = sink_ref[ha] * LOG2E
            sink_b = sink_ref[hb] * LOG2E
            ma = jnp.maximum(jnp.max(sa, axis=-1, keepdims=True), sink_a)
            mb = jnp.maximum(jnp.max(sb, axis=-1, keepdims=True), sink_b)
            ps.append(jnp.concatenate([jnp.exp2(sa - ma), jnp.exp2(sb - mb)], axis=1).astype(BF16))
            sink_terms.append(jnp.where(lane < hd, jnp.exp2(sink_a - ma), jnp.exp2(sink_b - mb)))
        return jnp.concatenate(ps, axis=0), sink_terms

    def weigh(mp, p_all, sink_terms):
        vp = vblk[:, mp * 2 * hd:(mp + 1) * 2 * hd]
        v2 = jnp.concatenate([jnp.concatenate([jnp.where(lane < hd, vp, zero), ind_a], axis=1),
                              jnp.concatenate([jnp.where(lane >= hd, vp, zero), ind_b], axis=1)], axis=0)
        o_all = jnp.dot(p_all, v2, preferred_element_type=F32)
        for g, cols in enumerate(pair_cols(mp)):
            og = o_all[g * SWA_BLOCK:(g + 1) * SWA_BLOCK]
            o_ref[0, :, cols] = (og[:, :2 * hd] / (og[:, 2 * hd:] + sink_terms[g])).astype(BF16)

    for mp in range(SWA_KV_HEADS // 2):
        weigh(mp, *softmax(mp, score(mp)))


def _swa_attention(q, k, v, pos_col, pos_row, sinks, seq):
    b = q.shape[0]
    nq = q.shape[2]
    nkv = k.shape[2]
    return pl.pallas_call(
        functools.partial(_swa_attn_body, seq=seq),
        grid=(b, seq // SWA_BLOCK),
        in_specs=[
            pl.BlockSpec(memory_space=pltpu.SMEM),
            pl.BlockSpec((1, SWA_BLOCK, nq), lambda bi, j: (bi, j, 0)),
            pl.BlockSpec((1, seq, nkv), lambda bi, j: (bi, 0, 0)),
            pl.BlockSpec((1, seq, nkv), lambda bi, j: (bi, 0, 0)),
            pl.BlockSpec((SWA_BLOCK, 1), lambda bi, j: (j, 0)),
            pl.BlockSpec((1, seq), lambda bi, j: (0, 0)),
        ],
        out_specs=pl.BlockSpec((1, SWA_BLOCK, nq), lambda bi, j: (bi, j, 0)),
        out_shape=jax.ShapeDtypeStruct(q.shape, BF16),
        compiler_params=_params("arbitrary", "arbitrary"),
        name="swa_attention",
    )(sinks, q, k, v, pos_col, pos_row)


def _rope64(z, c, s1, s2):
    return z * c + pltpu.roll(z, V7X_LANES - MLA_ROPE // 2, 1) * s1 + pltpu.roll(z, MLA_ROPE // 2, 1) * s2


def _slot_rms(z, gain):
    ms = jnp.sum(z * z, axis=-1, keepdims=True) * (1.0 / MLA_ROPE)
    return z * lax.rsqrt(ms + EPS) * gain


def _mla_proj_body(x_ref, g_ref, wd_ref, qa_ref, kva_ref, kr_ref, c_ref, s1_ref, s2_ref,
                   wuqt_ref, wuk_ref, wuvt_ref, qn_ref, qr_ref, kn_ref, cos_ref, sin_ref,
                   qt_ref, k_ref, vt_ref):
    hn = _rms(x_ref[...], g_ref[...]).astype(BF16)
    d = jnp.dot(hn, wd_ref[...], preferred_element_type=F32)
    cq = _rms(d[:, :MLA_Q_RANK], qa_ref[...]).astype(BF16)
    ckv = _rms(d[:, MLA_Q_RANK:MLA_Q_RANK + MLA_KV_RANK], kva_ref[...]).astype(BF16)
    kp = _slot_rms(d[:, MLA_Q_RANK + MLA_KV_RANK:], kr_ref[...])
    kpe = _rope64(kp, c_ref[...], s1_ref[...], s2_ref[...]).astype(BF16)

    nt = (((1,), (1,)), ((), ()))
    qt = lax.dot_general(wuqt_ref[...], cq, nt, preferred_element_type=F32)
    vt = lax.dot_general(wuvt_ref[...], ckv, nt, preferred_element_type=F32)
    kn = jnp.dot(ckv, wuk_ref[...], preferred_element_type=F32)
    qng = qn_ref[...]
    qrg = qr_ref[...]
    kng = kn_ref[...]
    cos, sin = cos_ref[...], sin_ref[...]
    tm = ckv.shape[0]
    half = MLA_ROPE // 2
    ones = jnp.ones((MLA_VT_ROWS - MLA_V, tm), BF16)
    zeros = jnp.zeros((MLA_SLOT - MLA_NOPE - MLA_ROPE, tm), BF16)
    for h in range(MLA_HEADS):
        lo = h * MLA_SLOT
        mid = lo + MLA_NOPE
        qn = qt[lo:mid]
        ms = jnp.mean(qn * qn, axis=0, keepdims=True)
        qt_ref[0, h, 0:MLA_NOPE, :] = (qn * lax.rsqrt(ms + EPS) * qng).astype(BF16)
        x1 = qt[mid:mid + half]
        x2 = qt[mid + half:mid + 2 * half]
        ms = (jnp.sum(x1 * x1, axis=0, keepdims=True)
              + jnp.sum(x2 * x2, axis=0, keepdims=True)) * (1.0 / MLA_ROPE)
        r = lax.rsqrt(ms + EPS)
        x1 = x1 * r * qrg[0:half]
        x2 = x2 * r * qrg[half:2 * half]
        qt_ref[0, h, MLA_NOPE:MLA_NOPE + half, :] = (x1 * cos - x2 * sin).astype(BF16)
        qt_ref[0, h, MLA_NOPE + half:MLA_NOPE + 2 * half, :] = (x2 * cos + x1 * sin).astype(BF16)
        qt_ref[0, h, MLA_NOPE + 2 * half:MLA_SLOT, :] = zeros
        k_ref[0, h, :, 0:MLA_NOPE] = _rms(kn[:, h * MLA_NOPE:(h + 1) * MLA_NOPE], kng).astype(BF16)
        k_ref[0, h, :, MLA_NOPE:MLA_SLOT] = kpe
        vt_ref[0, h, 0:MLA_V, :] = vt[h * MLA_V:(h + 1) * MLA_V, :].astype(BF16)
        vt_ref[0, h, MLA_V:MLA_VT_ROWS, :] = ones


def _mla_proj(x, gain, w_down, qa, kva, kr, c, s1, s2, w_uqt, w_uk, w_uvt, qn, qr, kn, cos_t, sin_t, b, seq, tm):
    d = x.shape[1]
    nt = seq // tm
    full = lambda a: pl.BlockSpec(a.shape, lambda bi, i: (0,) * a.ndim)
    tab = pl.BlockSpec((tm, V7X_LANES), lambda bi, i: (i, 0))
    tab_t = pl.BlockSpec((MLA_ROPE // 2, tm), lambda bi, i: (0, i))
    out = pl.BlockSpec((1, MLA_HEADS, tm, MLA_SLOT), lambda bi, i: (bi, 0, i, 0))
    shape = jax.ShapeDtypeStruct((b, MLA_HEADS, seq, MLA_SLOT), BF16)
    out_q = pl.BlockSpec((1, MLA_HEADS, MLA_SLOT, tm), lambda bi, i: (bi, 0, 0, i))
    shape_q = jax.ShapeDtypeStruct((b, MLA_HEADS, MLA_SLOT, seq), BF16)
    out_v = pl.BlockSpec((1, MLA_HEADS, MLA_VT_ROWS, tm), lambda bi, i: (bi, 0, 0, i))
    shape_v = jax.ShapeDtypeStruct((b, MLA_HEADS, MLA_VT_ROWS, seq), BF16)
    return pl.pallas_call(
        _mla_proj_body,
        grid=(b, nt),
        in_specs=[pl.BlockSpec((tm, d), lambda bi, i: (bi * nt + i, 0)),
                  full(gain), full(w_down), full(qa), full(kva), full(kr), tab, tab, tab,
                  full(w_uqt), full(w_uk), full(w_uvt), full(qn), full(qr), full(kn), tab_t, tab_t],
        out_specs=[out_q, out, out_v],
        out_shape=[shape_q, shape, shape_v],
        compiler_params=_params("arbitrary", "arbitrary"),
        name="mla_proj",
    )(x, gain, w_down, qa, kva, kr, c, s1, s2, w_uqt, w_uk, w_uvt, qn, qr, kn, cos_t, sin_t)


def _mla_attn_body(q_ref, k_ref, vt_ref, o_ref, *scratch):
    step = pl.program_id(0)
    n_sub = len(scratch) // 4
    s_bufs, m_bufs = scratch[:2 * n_sub], scratch[2 * n_sub:]
    sub = s_bufs[0].shape[1]

    @pl.when(step == 0)
    def _():
        for r in s_bufs[n_sub:] + m_bufs[n_sub:]:
            r[...] = jnp.zeros(r.shape, F32)

    def work(t, s_new, m_new, s_old, m_old):
        cols = slice(t * sub, (t + 1) * sub)
        q = q_ref[0, 0, :, cols]
        m_prev = m_old[...]
        nc = s_new.shape[0] // MLA_KEY_CHUNK
        chunk = lambda c: slice(c * MLA_KEY_CHUNK, (c + 1) * MLA_KEY_CHUNK)

        def probs(c):
            return jnp.exp2(s_old[chunk(c), :] - m_prev).astype(BF16)

        def weigh(c, p, ot):
            part = jnp.dot(vt_ref[0, 0, :, chunk(c)], p, preferred_element_type=F32)
            return part if ot is None else ot + part

        def score(c, m_run):
            st = jnp.dot(k_ref[0, 0, chunk(c), :], q, preferred_element_type=F32)
            s_new[chunk(c), :] = st
            mc = jnp.max(st, axis=0, keepdims=True)
            return mc if m_run is None else jnp.maximum(m_run, mc)

        ot = weigh(0, probs(0), None)
        m_run = None
        for c in range(nc - 1):
            m_run = score(c, m_run)
            ot = weigh(c + 1, probs(c + 1), ot)
        o = ot[0:MLA_V] / ot[MLA_V:MLA_V + 1]
        o_ref[0, cols, :] = o.T.astype(BF16)
        m_new[...] = score(nc - 1, m_run)

    def run(new, old):
        for t in range(n_sub):
            work(t, s_bufs[new * n_sub + t], m_bufs[new * n_sub + t], s_bufs[old * n_sub + t], m_bufs[old * n_sub + t])

    @pl.when(step % 2 == 0)
    def _():
        run(0, 1)

    @pl.when(step % 2 == 1)
    def _():
        run(1, 0)


def _mla_attention(qt, k, vt, tq):
    b, nh, seq, slot = k.shape
    nq = seq // tq
    n = b * nh * nq
    n_sub = tq // MLA_SUB_TILE

    def tile(t):
        return t // (nh * nq), (t // nq) % nh, t % nq

    def q_map(s):
        bi, h, i = tile(jnp.minimum(s, n - 1))
        return bi, h, 0, i

    def k_map(s):
        bi, h, _ = tile(jnp.minimum(s, n - 1))
        return bi, h, 0, 0

    def vt_map(s):
        bi, h, _ = tile(jnp.maximum(s - 1, 0))
        return bi, h, 0, 0

    def o_map(s):
        bi, h, i = tile(jnp.maximum(s - 1, 0))
        return bi, i, h

    return pl.pallas_call(
        _mla_attn_body,
        grid=(n + 1,),
        in_specs=[
            pl.BlockSpec((1, 1, slot, tq), q_map),
            pl.BlockSpec((1, 1, seq, slot), k_map),
            pl.BlockSpec((1, 1, MLA_VT_ROWS, seq), vt_map),
        ],
        out_specs=pl.BlockSpec((1, tq, MLA_V), o_map),
        out_shape=jax.ShapeDtypeStruct((b, seq, nh * MLA_V), BF16),
        scratch_shapes=([pltpu.VMEM((seq, MLA_SUB_TILE), F32)] * (2 * n_sub)
                        + [pltpu.VMEM((1, MLA_SUB_TILE), F32)] * (2 * n_sub)),
        compiler_params=_params("arbitrary"),
        name="mla_attention",
    )(qt, k, vt)


def _swa_pair_perm(w_q_cols):
    lead = w_q_cols.shape[:-1]
    w = w_q_cols.reshape(lead + (2, 2, SWA_GROUP, SWA_HEAD_DIM))
    w = jnp.swapaxes(w, -3, -2)
    return w.reshape(lead + (SWA_HEADS * SWA_HEAD_DIM,))


def _rope_tables(positions):
    half = MLA_ROPE // 2
    inv = ROPE_THETA ** (-jnp.arange(0, MLA_ROPE, 2, dtype=F32) / MLA_ROPE)
    ang = positions.astype(F32)[:, None] * inv[None, :]
    cos, sin = jnp.cos(ang), jnp.sin(ang)
    z = jnp.zeros_like(cos)
    pad = jnp.zeros((positions.shape[0], V7X_LANES - MLA_ROPE), F32)
    c = jnp.concatenate([cos, cos, pad], axis=1)
    s1 = jnp.concatenate([-sin, z, pad], axis=1)
    s2 = jnp.concatenate([z, sin, pad], axis=1)
    return c, s1, s2, cos.T, sin.T


def _pad_lanes(v, n):
    return jnp.pad(v, ((0, 0), (0, n - v.shape[1])))


def kernel(x, positions, norm_mix_g, norm_ffn_g, pool_w, pool_scale, swa_w_qkv, swa_q_gain, swa_k_gain, swa_sinks, swa_w_o, mla_w_down, mla_q_a_gain, mla_kv_a_gain, mla_w_uq, mla_w_ukv, mla_qn_gain, mla_qr_gain, mla_kn_gain, mla_kr_gain, mla_w_o, ffn_w_in, ffn_conv_w, ffn_conv_b, ffn_w_out):
    b, seq, d = x.shape
    m = b * seq
    depth = norm_mix_g.shape[0]
    t = _tiles(m, seq)
    xf = x.reshape(m, d)

    f = ffn_w_out.shape[1]
    cwb = jnp.concatenate([ffn_conv_w, ffn_conv_b[:, None, :], jnp.zeros((depth, 4, f), F32)], axis=1)
    pool_wb = pool_w.astype(BF16)
    pool_bands = jnp.asarray(_pool_bands(), dtype=BF16)

    nq = SWA_HEADS * SWA_HEAD_DIM
    pos_col = positions.reshape(seq, 1)
    pos_row = positions.reshape(1, seq)
    blk = np.kron(np.eye(4, dtype=np.float32), np.ones((SWA_HEAD_DIM, SWA_HEAD_DIM), np.float32))
    e_blk = jnp.asarray(blk, dtype=BF16)
    c_tab, s1_tab, s2_tab, cos_t, sin_t = _rope_tables(positions)

    for i in range(depth):
        kind = i % N_MIXERS
        j = i // N_MIXERS
        gain = norm_mix_g[i][None, :]
        if kind == 0:
            xf = _pool_layer(xf, gain, pool_bands, pool_wb, pool_scale[j][None, :], j, seq, t["pool_ts"])
        elif kind == 1:
            w = swa_w_qkv[j].astype(BF16)
            w_qkv = jnp.concatenate([_swa_pair_perm(w[:, :nq]), w[:, nq:]], axis=1)
            w_o = _swa_pair_perm(swa_w_o[j].astype(BF16).T).T
            qg = jnp.tile(swa_q_gain[j], 4)[None, :]
            kg = jnp.tile(swa_k_gain[j], 4)[None, :]
            q, k, v = _swa_qkv(xf, gain, w_qkv, qg, kg, e_blk, t["proj_tm"])
            o = _swa_attention(q.reshape(b, seq, -1), k.reshape(b, seq, -1), v.reshape(b, seq, -1),
                               pos_col, pos_row, swa_sinks[j], seq)
            xf = _proj_residual(o.reshape(m, -1), w_o, xf, t["res_tm"])
        else:
            n_lat = MLA_Q_RANK + MLA_KV_RANK
            w_down = _pad_lanes(mla_w_down[j].astype(BF16), n_lat + V7X_LANES)
            wq = mla_w_uq[j].astype(BF16).reshape(MLA_Q_RANK, MLA_HEADS, MLA_NOPE + MLA_ROPE)
            wq = jnp.pad(wq, ((0, 0), (0, 0), (0, MLA_SLOT - MLA_NOPE - MLA_ROPE)))
            w_uqt = wq.reshape(MLA_Q_RANK, MLA_HEADS * MLA_SLOT).T
            wkv = mla_w_ukv[j].astype(BF16).reshape(MLA_KV_RANK, MLA_HEADS, MLA_NOPE + MLA_V)
            w_uk = wkv[:, :, :MLA_NOPE].reshape(MLA_KV_RANK, MLA_HEADS * MLA_NOPE)
            w_uvt = wkv[:, :, MLA_NOPE:].reshape(MLA_KV_RANK, MLA_HEADS * MLA_V).T
            up_tm = t["up_tm"]
            q_scale = (MLA_NOPE + MLA_ROPE) ** -0.5 * LOG2E
            qn = jnp.broadcast_to((mla_qn_gain[j] * q_scale)[:, None], (MLA_NOPE, up_tm))
            qr = jnp.broadcast_to((mla_qr_gain[j] * q_scale)[:, None], (MLA_ROPE, up_tm))
            kr = _pad_lanes(mla_kr_gain[j][None, :], V7X_LANES)
            qt, k, vt = _mla_proj(xf, gain, w_down, mla_q_a_gain[j][None, :], mla_kv_a_gain[j][None, :], kr,
                                  c_tab, s1_tab, s2_tab, w_uqt, w_uk, w_uvt, qn, qr, mla_kn_gain[j][None, :],
                                  cos_t, sin_t, b, seq, up_tm)
            o = _mla_attention(qt, k, vt, t["attn_tq"])
            xf = _proj_residual(o.reshape(m, -1), mla_w_o[j].astype(BF16), xf, t["res_tm"])
        xf = _ffn_layer(xf, norm_ffn_g[i][None, :], ffn_w_in, cwb, ffn_w_out, i, seq, t["ffn_tm"], t["ffn_tf"])
    return xf.reshape(b, seq, d)
```

```python
import functools

import numpy as np
import jax
import jax.numpy as jnp
from jax import lax
from jax.experimental import pallas as pl
from jax.experimental.pallas import tpu as pltpu

F32 = jnp.float32
BF16 = jnp.bfloat16

EPS = 1e-6
LOG2E = 1.4426950408889634
N_MIXERS = 3
POOL_WINDOWS = (2, 4, 8, 16)
POOL_HALO = 16
POOL_SUB = 128
POOL_BAND_K = 256
SWA_HEADS = 32
SWA_KV_HEADS = 4
SWA_GROUP = SWA_HEADS // SWA_KV_HEADS
SWA_HEAD_DIM = 64
SWA_WINDOW = 128
SWA_BLOCK = 128
SWA_SPAN = SWA_BLOCK + 2 * SWA_WINDOW
MLA_HEADS = 16
MLA_NOPE = 128
MLA_ROPE = 64
MLA_V = 128
MLA_Q_RANK = 512
MLA_KV_RANK = 512
MLA_SLOT = 256
MLA_VT_ROWS = MLA_V + 16
MLA_KEY_CHUNK = 512
MLA_SUB_TILE = 512
ROPE_THETA = 10000.0

V7X_LANES = 128
V7X_SUBLANES = 8
V7X_VMEM_BYTES = 64 * 1024 * 1024
V7X_VMEM_LIMIT_BYTES = 56 * 1024 * 1024


def _tiles(m, seq):
    def fit(t):
        while seq % t:
            t //= 2
        return t
    return dict(ffn_tm=fit(1024), ffn_tf=512, pool_ts=fit(1024), proj_tm=fit(1024), up_tm=fit(256),
                attn_tq=fit(1024), res_tm=fit(1024))


def _params(*sem, vmem=V7X_VMEM_LIMIT_BYTES):
    return pltpu.CompilerParams(dimension_semantics=sem, vmem_limit_bytes=vmem)


def _rms(x, g):
    ms = jnp.mean(x * x, axis=-1, keepdims=True)
    return x * lax.rsqrt(ms + EPS) * g


def _halo_specs(tm, m, d, nargs, rows=V7X_SUBLANES):
    r = tm // rows
    last = m // rows - 1
    if nargs == 1:
        prev = pl.BlockSpec((rows, d), lambda i: (jnp.maximum(i * r - 1, 0), 0))
        nxt = pl.BlockSpec((rows, d), lambda i: (jnp.minimum((i + 1) * r, last), 0))
    else:
        prev = pl.BlockSpec((rows, d), lambda i, j: (jnp.maximum(i * r - 1, 0), 0))
        nxt = pl.BlockSpec((rows, d), lambda i, j: (jnp.minimum((i + 1) * r, last), 0))
    return prev, nxt


def _ffn_norm_tile(xp_ref, x_ref, xn_ref, g_ref, hn_ref, first_row, *, tm, seq):
    g = g_ref[...]
    hn_ref[0:tm, :] = _rms(x_ref[...], g).astype(BF16)
    s0 = first_row % seq
    keep_prev = jnp.where(s0 != 0, 7, -1)
    keep_next = jnp.where(s0 + tm != seq, 8, -1)
    ext = jnp.concatenate([_rms(xp_ref[...], g), _rms(xn_ref[...], g)], axis=0)
    r = lax.broadcasted_iota(jnp.int32, (16, 1), 0)
    ext = jnp.where((r == keep_prev) | (r == keep_next), ext, 0.0)
    hn_ref[tm:tm + 16, :] = ext.astype(BF16)


def _ffn_body(xp_ref, x_ref, xn_ref, g_ref, wg_ref, wv_ref, cwb_ref, wo_ref, o_ref, hn_ref, *, tm, seq):
    i = pl.program_id(0)
    j = pl.program_id(1)

    @pl.when(j == 0)
    def _():
        _ffn_norm_tile(xp_ref, x_ref, xn_ref, g_ref, hn_ref, i * tm, tm=tm, seq=seq)

    ug = jnp.dot(hn_ref[...], wg_ref[...].astype(BF16), preferred_element_type=F32)
    uv = jnp.dot(hn_ref[0:tm, :], wv_ref[...].astype(BF16), preferred_element_type=F32)
    gm = ug[0:tm]
    g_prev = ug[tm + 7:tm + 8]
    g_next = ug[tm + 8:tm + 9]
    rows = lax.broadcasted_iota(jnp.int32, (tm, 1), 0)
    up = jnp.where(rows == 0, g_prev, pltpu.roll(gm, 1, 0))
    dn = jnp.where(rows == tm - 1, g_next, pltpu.roll(gm, tm - 1, 0))
    cwb = cwb_ref[...]
    gc = up * cwb[0:1] + gm * cwb[1:2] + dn * cwb[2:3] + cwb[3:4]
    act = (gc * jax.nn.sigmoid(gc) * uv).astype(BF16)
    acc = jnp.where(j == 0, x_ref[...], o_ref[...])
    o_ref[...] = acc + jnp.dot(act, wo_ref[...].astype(BF16), preferred_element_type=F32)


def _ffn_layer(x, gain, w_in, cwb, w_out, layer, seq, tm, tf):
    m, d = x.shape
    f = w_out.shape[1]
    nf = f // tf
    prev, nxt = _halo_specs(tm, m, d, 2)
    return pl.pallas_call(
        functools.partial(_ffn_body, tm=tm, seq=seq),
        grid=(m // tm, nf),
        in_specs=[
            prev,
            pl.BlockSpec((tm, d), lambda i, j: (i, 0)),
            nxt,
            pl.BlockSpec((1, d), lambda i, j: (0, 0)),
            pl.BlockSpec((None, d, tf), lambda i, j: (layer, 0, j)),
            pl.BlockSpec((None, d, tf), lambda i, j: (layer, 0, j + nf)),
            pl.BlockSpec((None, 8, tf), lambda i, j: (layer, 0, j)),
            pl.BlockSpec((None, tf, d), lambda i, j: (layer, j, 0)),
        ],
        out_specs=pl.BlockSpec((tm, d), lambda i, j: (i, 0)),
        out_shape=jax.ShapeDtypeStruct((m, d), F32),
        scratch_shapes=[pltpu.VMEM((tm + 16, d), BF16)],
        compiler_params=_params("arbitrary", "arbitrary", vmem=V7X_VMEM_BYTES),
        name="ffn_convglu",
    )(x, x, x, gain, w_in, w_in, cwb, w_out)


def _pool_bands():
    t = np.arange(POOL_SUB)[:, None]
    s = np.arange(POOL_BAND_K)[None, :]
    bands = [((s >= t + POOL_HALO - w // 2) & (s <= t + POOL_HALO + (w - 1 - w // 2))) for w in POOL_WINDOWS]
    return np.stack(bands).astype(np.float32)


def _pool_body(xp_ref, x_ref, xn_ref, g_ref, band_ref, w_ref, sc_ref, o_ref, hb_ref, *, ts, seq):
    i = pl.program_id(0)
    data_rows = ts + 2 * POOL_HALO

    @pl.when(i == 0)
    def _():
        hb_ref[data_rows:, :] = jnp.zeros((hb_ref.shape[0] - data_rows, hb_ref.shape[1]), BF16)

    g = g_ref[...]
    x = x_ref[...]
    s0 = (i * ts) % seq
    h = _rms(x, g)

    hb_ref[0:POOL_HALO, :] = (_rms(xp_ref[...], g) * jnp.where(s0 != 0, 1.0, 0.0)).astype(BF16)
    hb_ref[POOL_HALO:POOL_HALO + ts, :] = h.astype(BF16)
    hb_ref[POOL_HALO + ts:data_rows, :] = (_rms(xn_ref[...], g) * jnp.where(s0 + ts != seq, 1.0, 0.0)).astype(BF16)

    pos = s0 + lax.broadcasted_iota(jnp.int32, (ts, 1), 0)
    dg = x.shape[1] // len(POOL_WINDOWS)
    for gi, w in enumerate(POOL_WINDOWS):
        left = w // 2
        right = w - 1 - left
        cols = slice(gi * dg, (gi + 1) * dg)
        band = band_ref[gi]
        wsum = []
        for r0 in range(0, ts, POOL_SUB):
            rows = slice(r0, r0 + POOL_BAND_K)
            wsum.append(jnp.dot(band, hb_ref[rows, cols], preferred_element_type=F32))
        wsum = jnp.concatenate(wsum, axis=0)
        cnt = (jnp.minimum(pos + right + 1, seq) - jnp.maximum(pos - left, 0)).astype(F32)
        pooled = (wsum * (1.0 / cnt) - h[:, cols]).astype(BF16)
        y = jnp.dot(pooled, w_ref[gi], preferred_element_type=F32) * sc_ref[:, cols]
        o_ref[:, cols] = x[:, cols] + y


def _pool_layer(x, gain, bands, pool_w, scale, layer, seq, ts):
    m, d = x.shape
    ng, dg = pool_w.shape[1], pool_w.shape[2]
    prev, nxt = _halo_specs(ts, m, d, 1, POOL_HALO)
    scratch = pltpu.VMEM((ts + POOL_BAND_K - POOL_SUB, d), BF16)
    return pl.pallas_call(
        functools.partial(_pool_body, ts=ts, seq=seq),
        grid=(m // ts,),
        in_specs=[
            prev,
            pl.BlockSpec((ts, d), lambda i: (i, 0)),
            nxt,
            pl.BlockSpec((1, d), lambda i: (0, 0)),
            pl.BlockSpec(bands.shape, lambda i: (0, 0, 0)),
            pl.BlockSpec((None, ng, dg, dg), lambda i: (layer, 0, 0, 0)),
            pl.BlockSpec((1, d), lambda i: (0, 0)),
        ],
        out_specs=pl.BlockSpec((ts, d), lambda i: (i, 0)),
        out_shape=jax.ShapeDtypeStruct((m, d), F32),
        scratch_shapes=[scratch],
        compiler_params=_params("arbitrary"),
        name="pool_mixer",
    )(x, x, x, gain, bands, pool_w, scale)


def _proj_res_body(a_ref, w_ref, x_ref, o_ref):
    o_ref[...] = x_ref[...] + jnp.dot(a_ref[...], w_ref[...], preferred_element_type=F32)


def _proj_residual(a, w, x, tm):
    m, d = x.shape
    k = a.shape[1]
    return pl.pallas_call(
        _proj_res_body,
        grid=(m // tm,),
        in_specs=[
            pl.BlockSpec((tm, k), lambda i: (i, 0)),
            pl.BlockSpec((k, d), lambda i: (0, 0)),
            pl.BlockSpec((tm, d), lambda i: (i, 0)),
        ],
        out_specs=pl.BlockSpec((tm, d), lambda i: (i, 0)),
        out_shape=jax.ShapeDtypeStruct((m, d), F32),
        compiler_params=_params("arbitrary"),
        name="proj_residual",
    )(a, w, x)


def _group_sumsq(z, e):
    return jnp.dot((z * z).astype(BF16), e, preferred_element_type=F32)


def _swa_qkv_body(x_ref, g_ref, w_ref, qg_ref, kg_ref, e_ref, q_ref, k_ref, v_ref):
    hn = _rms(x_ref[...], g_ref[...]).astype(BF16)
    qkv = jnp.dot(hn, w_ref[...], preferred_element_type=F32)
    e = e_ref[...]
    nq = q_ref.shape[1]
    nkv = k_ref.shape[1]
    cw = e.shape[0]
    inv_hd = 1.0 / SWA_HEAD_DIM

    def head_norm(z, gain):
        return z * lax.rsqrt(_group_sumsq(z, e) * inv_hd + EPS) * gain

    qg = qg_ref[...] * (SWA_HEAD_DIM ** -0.5 * LOG2E)
    for c in range(nq // cw):
        cols = slice(c * cw, (c + 1) * cw)
        q_ref[:, cols] = head_norm(qkv[:, cols], qg).astype(BF16)
    k_ref[...] = head_norm(qkv[:, nq:nq + nkv], kg_ref[...]).astype(BF16)
    v_ref[...] = qkv[:, nq + nkv:].astype(BF16)


def _swa_qkv(x, gain, w_qkv, qg, kg, e, tm):
    m, d = x.shape
    nkv = SWA_KV_HEADS * SWA_HEAD_DIM
    nq = w_qkv.shape[1] - 2 * nkv
    return pl.pallas_call(
        _swa_qkv_body,
        grid=(m // tm,),
        in_specs=[
            pl.BlockSpec((tm, d), lambda i: (i, 0)),
            pl.BlockSpec((1, d), lambda i: (0, 0)),
            pl.BlockSpec(w_qkv.shape, lambda i: (0, 0)),
            pl.BlockSpec((1, nkv), lambda i: (0, 0)),
            pl.BlockSpec((1, nkv), lambda i: (0, 0)),
            pl.BlockSpec(e.shape, lambda i: (0, 0)),
        ],
        out_specs=[
            pl.BlockSpec((tm, nq), lambda i: (i, 0)),
            pl.BlockSpec((tm, nkv), lambda i: (i, 0)),
            pl.BlockSpec((tm, nkv), lambda i: (i, 0)),
        ],
        out_shape=[
            jax.ShapeDtypeStruct((m, nq), BF16),
            jax.ShapeDtypeStruct((m, nkv), BF16),
            jax.ShapeDtypeStruct((m, nkv), BF16),
        ],
        compiler_params=_params("arbitrary"),
        name="swa_qkv",
    )(x, gain, w_qkv, qg, kg, e)


def _swa_slopes():
    return [float(2.0 ** (-8.0 * (i + 1) / SWA_HEADS)) for i in range(SWA_HEADS)]


def _swa_attn_body(sink_ref, q_ref, k_ref, v_ref, pc_ref, pr_ref, o_ref, *, seq):
    j = pl.program_id(1)
    hd = SWA_HEAD_DIM
    start = pl.multiple_of(jnp.clip(j * SWA_BLOCK - SWA_WINDOW, 0, seq - SWA_SPAN), SWA_BLOCK)
    kblk = k_ref[0, pl.ds(start, SWA_SPAN), :]
    vblk = v_ref[0, pl.ds(start, SWA_SPAN), :]
    pk = pr_ref[:, pl.ds(start, SWA_SPAN)]
    pq = pc_ref[...]
    qi = j * SWA_BLOCK + lax.broadcasted_iota(jnp.int32, (SWA_BLOCK, 1), 0)
    ki = start + lax.broadcasted_iota(jnp.int32, (1, SWA_SPAN), 1)
    in_win = jnp.abs(qi - ki) <= SWA_WINDOW
    dist = jnp.where(in_win, jnp.abs(pq - pk).astype(F32), jnp.inf)
    lane = lax.broadcasted_iota(jnp.int32, (1, 2 * hd), 1)
    slopes = _swa_slopes()
    zero = jnp.zeros((), BF16)
    lane_full = lax.broadcasted_iota(jnp.int32, (SWA_SPAN, 2 * hd), 1)
    ind_a = jnp.where(lane_full < hd, 1.0, 0.0).astype(BF16)
    ind_b = jnp.where(lane_full >= hd, 1.0, 0.0).astype(BF16)
    def pair_cols(mp):
        return [slice((mp * SWA_GROUP + g) * 2 * hd, (mp * SWA_GROUP + g + 1) * 2 * hd) for g in range(SWA_GROUP)]

    def score(mp):
        kp = kblk[:, mp * 2 * hd:(mp + 1) * 2 * hd]
        k2 = jnp.concatenate([jnp.where(lane < hd, kp, zero), jnp.where(lane >= hd, kp, zero)], axis=0)
        q_all = jnp.concatenate([q_ref[0, :, cols] for cols in pair_cols(mp)], axis=0)
        return lax.dot_general(q_all, k2, (((1,), (1,)), ((), ())), preferred_element_type=F32)

    def softmax(mp, s_all):
        ps, sink_terms = [], []
        for g in range(SWA_GROUP):
            ha = (2 * mp) * SWA_GROUP + g
            hb = (2 * mp + 1) * SWA_GROUP + g
            s = s_all[g * SWA_BLOCK:(g + 1) * SWA_BLOCK]
            sa = s[:, :SWA_SPAN] - (LOG2E * slopes[ha]) * dist
            sb = s[:, SWA_SPAN:] - (LOG2E * slopes[hb]) * dist
            sink_a = sink_ref[ha] * LOG2E
            sink_b = sink_ref[hb] * LOG2E
            ma = jnp.maximum(jnp.max(sa, axis=-1, keepdims=True), sink_a)
            mb = jnp.maximum(jnp.max(sb, axis=-1, keepdims=True), sink_b)
            ps.append(jnp.concatenate([jnp.exp2(sa - ma), jnp.exp2(sb - mb)], axis=1).astype(BF16))
            sink_terms.append(jnp.where(lane < hd, jnp.exp2(sink_a - ma), jnp.exp2(sink_b - mb)))
        return jnp.concatenate(ps, axis=0), sink_terms

    def weigh(mp, p_all, sink_terms):
        vp = vblk[:, mp * 2 * hd:(mp + 1) * 2 * hd]
        v2 = jnp.concatenate([jnp.concatenate([jnp.where(lane < hd, vp, zero), ind_a], axis=1),
                              jnp.concatenate([jnp.where(lane >= hd, vp, zero), ind_b], axis=1)], axis=0)
        o_all = jnp.dot(p_all, v2, preferred_element_type=F32)
        for g, cols in enumerate(pair_cols(mp)):
            og = o_all[g * SWA_BLOCK:(g + 1) * SWA_BLOCK]
            o_ref[0, :, cols] = (og[:, :2 * hd] / (og[:, 2 * hd:] + sink_terms[g])).astype(BF16)

    for mp in range(SWA_KV_HEADS // 2):
        weigh(mp, *softmax(mp, score(mp)))


def _swa_attention(q, k, v, pos_col, pos_row, sinks, seq):
    b = q.shape[0]
    nq = q.shape[2]
    nkv = k.shape[2]
    return pl.pallas_call(
        functools.partial(_swa_attn_body, seq=seq),
        grid=(b, seq // SWA_BLOCK),
        in_specs=[
            pl.BlockSpec(memory_space=pltpu.SMEM),
            pl.BlockSpec((1, SWA_BLOCK, nq), lambda bi, j: (bi, j, 0)),
            pl.BlockSpec((1, seq, nkv), lambda bi, j: (bi, 0, 0)),
            pl.BlockSpec((1, seq, nkv), lambda bi, j: (bi, 0, 0)),
            pl.BlockSpec((SWA_BLOCK, 1), lambda bi, j: (j, 0)),
            pl.BlockSpec((1, seq), lambda bi, j: (0, 0)),
        ],
        out_specs=pl.BlockSpec((1, SWA_BLOCK, nq), lambda bi, j: (bi, j, 0)),
        out_shape=jax.ShapeDtypeStruct(q.shape, BF16),
        compiler_params=_params("arbitrary", "arbitrary"),
        name="swa_attention",
    )(sinks, q, k, v, pos_col, pos_row)


def _rope64(z, c, s1, s2):
    return z * c + pltpu.roll(z, V7X_LANES - MLA_ROPE // 2, 1) * s1 + pltpu.roll(z, MLA_ROPE // 2, 1) * s2


def _slot_rms(z, gain):
    ms = jnp.sum(z * z, axis=-1, keepdims=True) * (1.0 / MLA_ROPE)
    return z * lax.rsqrt(ms + EPS) * gain


def _mla_proj_body(x_ref, g_ref, wd_ref, qa_ref, kva_ref, kr_ref, c_ref, s1_ref, s2_ref,
                   wuqt_ref, wuk_ref, wuvt_ref, qn_ref, qr_ref, kn_ref, cos_ref, sin_ref,
                   qt_ref, k_ref, vt_ref):
    hn = _rms(x_ref[...], g_ref[...]).astype(BF16)
    d = jnp.dot(hn, wd_ref[...], preferred_element_type=F32)
    cq = _rms(d[:, :MLA_Q_RANK], qa_ref[...]).astype(BF16)
    ckv = _rms(d[:, MLA_Q_RANK:MLA_Q_RANK + MLA_KV_RANK], kva_ref[...]).astype(BF16)
    kp = _slot_rms(d[:, MLA_Q_RANK + MLA_KV_RANK:], kr_ref[...])
    kpe = _rope64(kp, c_ref[...], s1_ref[...], s2_ref[...]).astype(BF16)

    nt = (((1,), (1,)), ((), ()))
    qt = lax.dot_general(wuqt_ref[...], cq, nt, preferred_element_type=F32)
    vt = lax.dot_general(wuvt_ref[...], ckv, nt, preferred_element_type=F32)
    kn = jnp.dot(ckv, wuk_ref[...], preferred_element_type=F32)
    qng = qn_ref[...]
    qrg = qr_ref[...]
    kng = kn_ref[...]
    cos, sin = cos_ref[...], sin_ref[...]
    tm = ckv.shape[0]
    half = MLA_ROPE // 2
    ones = jnp.ones((MLA_VT_ROWS - MLA_V, tm), BF16)
    zeros = jnp.zeros((MLA_SLOT - MLA_NOPE - MLA_ROPE, tm), BF16)
    for h in range(MLA_HEADS):
        lo = h * MLA_SLOT
        mid = lo + MLA_NOPE
        qn = qt[lo:mid]
        ms = jnp.mean(qn * qn, axis=0, keepdims=True)
        qt_ref[0, h, 0:MLA_NOPE, :] = (qn * lax.rsqrt(ms + EPS) * qng).astype(BF16)
        x1 = qt[mid:mid + half]
        x2 = qt[mid + half:mid + 2 * half]
        ms = (jnp.sum(x1 * x1, axis=0, keepdims=True)
              + jnp.sum(x2 * x2, axis=0, keepdims=True)) * (1.0 / MLA_ROPE)
        r = lax.rsqrt(ms + EPS)
        x1 = x1 * r * qrg[0:half]
        x2 = x2 * r * qrg[half:2 * half]
        qt_ref[0, h, MLA_NOPE:MLA_NOPE + half, :] = (x1 * cos - x2 * sin).astype(BF16)
        qt_ref[0, h, MLA_NOPE + half:MLA_NOPE + 2 * half, :] = (x2 * cos + x1 * sin).astype(BF16)
        qt_ref[0, h, MLA_NOPE + 2 * half:MLA_SLOT, :] = zeros
        k_ref[0, h, :, 0:MLA_NOPE] = _rms(kn[:, h * MLA_NOPE:(h + 1) * MLA_NOPE], kng).astype(BF16)
        k_ref[0, h, :, MLA_NOPE:MLA_SLOT] = kpe
        vt_ref[0, h, 0:MLA_V, :] = vt[h * MLA_V:(h + 1) * MLA_V, :].astype(BF16)
        vt_ref[0, h, MLA_V:MLA_VT_ROWS, :] = ones


def _mla_proj(x, gain, w_down, qa, kva, kr, c, s1, s2, w_uqt, w_uk, w_uvt, qn, qr, kn, cos_t, sin_t, b, seq, tm):
    d = x.shape[1]
    nt = seq // tm
    full = lambda a: pl.BlockSpec(a.shape, lambda bi, i: (0,) * a.ndim)
    tab = pl.BlockSpec((tm, V7X_LANES), lambda bi, i: (i, 0))
    tab_t = pl.BlockSpec((MLA_ROPE // 2, tm), lambda bi, i: (0, i))
    out = pl.BlockSpec((1, MLA_HEADS, tm, MLA_SLOT), lambda bi, i: (bi, 0, i, 0))
    shape = jax.ShapeDtypeStruct((b, MLA_HEADS, seq, MLA_SLOT), BF16)
    out_q = pl.BlockSpec((1, MLA_HEADS, MLA_SLOT, tm), lambda bi, i: (bi, 0, 0, i))
    shape_q = jax.ShapeDtypeStruct((b, MLA_HEADS, MLA_SLOT, seq), BF16)
    out_v = pl.BlockSpec((1, MLA_HEADS, MLA_VT_ROWS, tm), lambda bi, i: (bi, 0, 0, i))
    shape_v = jax.ShapeDtypeStruct((b, MLA_HEADS, MLA_VT_ROWS, seq), BF16)
    return pl.pallas_call(
        _mla_proj_body,
        grid=(b, nt),
        in_specs=[pl.BlockSpec((tm, d), lambda bi, i: (bi * nt + i, 0)),
                  full(gain), full(w_down), full(qa), full(kva), full(kr), tab, tab, tab,
                  full(w_uqt), full(w_uk), full(w_uvt), full(qn), full(qr), full(kn), tab_t, tab_t],
        out_specs=[out_q, out, out_v],
        out_shape=[shape_q, shape, shape_v],
        compiler_params=_params("arbitrary", "arbitrary"),
        name="mla_proj",
    )(x, gain, w_down, qa, kva, kr, c, s1, s2, w_uqt, w_uk, w_uvt, qn, qr, kn, cos_t, sin_t)


def _mla_attn_body(q_ref, k_ref, vt_ref, o_ref, *scratch):
    step = pl.program_id(0)
    n_sub = len(scratch) // 4
    s_bufs, m_bufs = scratch[:2 * n_sub], scratch[2 * n_sub:]
    sub = s_bufs[0].shape[1]

    @pl.when(step == 0)
    def _():
        for r in s_bufs[n_sub:] + m_bufs[n_sub:]:
            r[...] = jnp.zeros(r.shape, F32)

    def work(t, s_new, m_new, s_old, m_old):
        cols = slice(t * sub, (t + 1) * sub)
        q = q_ref[0, 0, :, cols]
        m_prev = m_old[...]
        nc = s_new.shape[0] // MLA_KEY_CHUNK
        chunk = lambda c: slice(c * MLA_KEY_CHUNK, (c + 1) * MLA_KEY_CHUNK)

        def probs(c):
            return jnp.exp2(s_old[chunk(c), :] - m_prev).astype(BF16)

        def weigh(c, p, ot):
            part = jnp.dot(vt_ref[0, 0, :, chunk(c)], p, preferred_element_type=F32)
            return part if ot is None else ot + part

        def score(c, m_run):
            st = jnp.dot(k_ref[0, 0, chunk(c), :], q, preferred_element_type=F32)
            s_new[chunk(c), :] = st
            mc = jnp.max(st, axis=0, keepdims=True)
            return mc if m_run is None else jnp.maximum(m_run, mc)

        ot = weigh(0, probs(0), None)
        m_run = None
        for c in range(nc - 1):
            m_run = score(c, m_run)
            ot = weigh(c + 1, probs(c + 1), ot)
        o = ot[0:MLA_V] / ot[MLA_V:MLA_V + 1]
        o_ref[0, cols, :] = o.T.astype(BF16)
        m_new[...] = score(nc - 1, m_run)

    def run(new, old):
        for t in range(n_sub):
            work(t, s_bufs[new * n_sub + t], m_bufs[new * n_sub + t], s_bufs[old * n_sub + t], m_bufs[old * n_sub + t])

    @pl.when(step % 2 == 0)
    def _():
        run(0, 1)

    @pl.when(step % 2 == 1)
    def _():
        run(1, 0)


def _mla_attention(qt, k, vt, tq):
    b, nh, seq, slot = k.shape
    nq = seq // tq
    n = b * nh * nq
    n_sub = tq // MLA_SUB_TILE

    def tile(t):
        return t // (nh * nq), (t // nq) % nh, t % nq

    def q_map(s):
        bi, h, i = tile(jnp.minimum(s, n - 1))
        return bi, h, 0, i

    def k_map(s):
        bi, h, _ = tile(jnp.minimum(s, n - 1))
        return bi, h, 0, 0

    def vt_map(s):
        bi, h, _ = tile(jnp.maximum(s - 1, 0))
        return bi, h, 0, 0

    def o_map(s):
        bi, h, i = tile(jnp.maximum(s - 1, 0))
        return bi, i, h

    return pl.pallas_call(
        _mla_attn_body,
        grid=(n + 1,),
        in_specs=[
            pl.BlockSpec((1, 1, slot, tq), q_map),
            pl.BlockSpec((1, 1, seq, slot), k_map),
            pl.BlockSpec((1, 1, MLA_VT_ROWS, seq), vt_map),
        ],
        out_specs=pl.BlockSpec((1, tq, MLA_V), o_map),
        out_shape=jax.ShapeDtypeStruct((b, seq, nh * MLA_V), BF16),
        scratch_shapes=([pltpu.VMEM((seq, MLA_SUB_TILE), F32)] * (2 * n_sub)
                        + [pltpu.VMEM((1, MLA_SUB_TILE), F32)] * (2 * n_sub)),
        compiler_params=_params("arbitrary"),
        name="mla_attention",
    )(qt, k, vt)


def _swa_pair_perm(w_q_cols):
    lead = w_q_cols.shape[:-1]
    w = w_q_cols.reshape(lead + (2, 2, SWA_GROUP, SWA_HEAD_DIM))
    w = jnp.swapaxes(w, -3, -2)
    return w.reshape(lead + (SWA_HEADS * SWA_HEAD_DIM,))


def _rope_tables(positions):
    half = MLA_ROPE // 2
    inv = ROPE_THETA ** (-jnp.arange(0, MLA_ROPE, 2, dtype=F32) / MLA_ROPE)
    ang = positions.astype(F32)[:, None] * inv[None, :]
    cos, sin = jnp.cos(ang), jnp.sin(ang)
    z = jnp.zeros_like(cos)
    pad = jnp.zeros((positions.shape[0], V7X_LANES - MLA_ROPE), F32)
    c = jnp.concatenate([cos, cos, pad], axis=1)
    s1 = jnp.concatenate([-sin, z, pad], axis=1)
    s2 = jnp.concatenate([z, sin, pad], axis=1)
    return c, s1, s2, cos.T, sin.T


def _pad_lanes(v, n):
    return jnp.pad(v, ((0, 0), (0, n - v.shape[1])))


def kernel(x, positions, norm_mix_g, norm_ffn_g, pool_w, pool_scale, swa_w_qkv, swa_q_gain, swa_k_gain, swa_sinks, swa_w_o, mla_w_down, mla_q_a_gain, mla_kv_a_gain, mla_w_uq, mla_w_ukv, mla_qn_gain, mla_qr_gain, mla_kn_gain, mla_kr_gain, mla_w_o, ffn_w_in, ffn_conv_w, ffn_conv_b, ffn_w_out):
    b, seq, d = x.shape
    m = b * seq
    depth = norm_mix_g.shape[0]
    t = _tiles(m, seq)
    xf = x.reshape(m, d)

    f = ffn_w_out.shape[1]
    cwb = jnp.concatenate([ffn_conv_w, ffn_conv_b[:, None, :], jnp.zeros((depth, 4, f), F32)], axis=1)
    pool_wb = pool_w.astype(BF16)
    pool_bands = jnp.asarray(_pool_bands(), dtype=BF16)

    nq = SWA_HEADS * SWA_HEAD_DIM
    pos_col = positions.reshape(seq, 1)
    pos_row = positions.reshape(1, seq)
    blk = np.kron(np.eye(4, dtype=np.float32), np.ones((SWA_HEAD_DIM, SWA_HEAD_DIM), np.float32))
    e_blk = jnp.asarray(blk, dtype=BF16)
    c_tab, s1_tab, s2_tab, cos_t, sin_t = _rope_tables(positions)

    for i in range(depth):
        kind = i % N_MIXERS
        j = i // N_MIXERS
        gain = norm_mix_g[i][None, :]
        if kind == 0:
            xf = _pool_layer(xf, gain, pool_bands, pool_wb, pool_scale[j][None, :], j, seq, t["pool_ts"])
        elif kind == 1:
            w = swa_w_qkv[j].astype(BF16)
            w_qkv = jnp.concatenate([_swa_pair_perm(w[:, :nq]), w[:, nq:]], axis=1)
            w_o = _swa_pair_perm(swa_w_o[j].astype(BF16).T).T
            qg = jnp.tile(swa_q_gain[j], 4)[None, :]
            kg = jnp.tile(swa_k_gain[j], 4)[None, :]
            q, k, v = _swa_qkv(xf, gain, w_qkv, qg, kg, e_blk, t["proj_tm"])
            o = _swa_attention(q.reshape(b, seq, -1), k.reshape(b, seq, -1), v.reshape(b, seq, -1),
                               pos_col, pos_row, swa_sinks[j], seq)
            xf = _proj_residual(o.reshape(m, -1), w_o, xf, t["res_tm"])
        else:
            n_lat = MLA_Q_RANK + MLA_KV_RANK
            w_down = _pad_lanes(mla_w_down[j].astype(BF16), n_lat + V7X_LANES)
            wq = mla_w_uq[j].astype(BF16).reshape(MLA_Q_RANK, MLA_HEADS, MLA_NOPE + MLA_ROPE)
            wq = jnp.pad(wq, ((0, 0), (0, 0), (0, MLA_SLOT - MLA_NOPE - MLA_ROPE)))
            w_uqt = wq.reshape(MLA_Q_RANK, MLA_HEADS * MLA_SLOT).T
            wkv = mla_w_ukv[j].astype(BF16).reshape(MLA_KV_RANK, MLA_HEADS, MLA_NOPE + MLA_V)
            w_uk = wkv[:, :, :MLA_NOPE].reshape(MLA_KV_RANK, MLA_HEADS * MLA_NOPE)
            w_uvt = wkv[:, :, MLA_NOPE:].reshape(MLA_KV_RANK, MLA_HEADS * MLA_V).T
            up_tm = t["up_tm"]
            q_scale = (MLA_NOPE + MLA_ROPE) ** -0.5 * LOG2E
            qn = jnp.broadcast_to((mla_qn_gain[j] * q_scale)[:, None], (MLA_NOPE, up_tm))
            qr = jnp.broadcast_to((mla_qr_gain[j] * q_scale)[:, None], (MLA_ROPE, up_tm))
            kr = _pad_lanes(mla_kr_gain[j][None, :], V7X_LANES)
            qt, k, vt = _mla_proj(xf, gain, w_down, mla_q_a_gain[j][None, :], mla_kv_a_gain[j][None, :], kr,
                                  c_tab, s1_tab, s2_tab, w_uqt, w_uk, w_uvt, qn, qr, mla_kn_gain[j][None, :],
                                  cos_t, sin_t, b, seq, up_tm)
            o = _mla_attention(qt, k, vt, t["attn_tq"])
            xf = _proj_residual(o.reshape(m, -1), mla_w_o[j].astype(BF16), xf, t["res_tm"])
        xf = _ffn_layer(xf, norm_ffn_g[i][None, :], ffn_w_in, cwb, ffn_w_out, i, seq, t["ffn_tm"], t["ffn_tf"])
    return xf.reshape(b, seq, d)
```
